```python
import math
import jax, jax.numpy as jnp
from jax import lax
import numpy as np

D_MODEL = 1024
BATCH = 32
SEQ = 2048
DEPTH = 2

N_EVEN = (DEPTH + 1) // 2
N_ODD = DEPTH // 2
NORM_EPS = 1e-6

A_HEADS = 4
A_DK = 128
A_DV = 128
A_WIDTH = A_HEADS * A_DK
A_CHUNK = 32
B_HEADS = 4
B_HD = 128
B_WIDTH = B_HEADS * B_HD
B_DILATIONS = ((128, 1), (512, 4), (2048, 16))
ROPE_THETA = 10000.0
IN_AB_WIDTH = 4 * A_WIDTH + 3 * B_WIDTH
MIX_WIDTH = A_WIDTH + B_WIDTH
C_GROUP = 16
C_GROUPS = D_MODEL // C_GROUP
C_STATE = 64
C_CHUNK = 128
C_MIN_NEG_RE = -1e-4
MEM_LEN = 256
X_HEADS = 4
X_HD = D_MODEL // X_HEADS
D_FF = -(-8 * D_MODEL // (3 * 256)) * 256

kernel_name = "hybrid_hgrn2_dilated_s5_block"


def rms_norm(x, w):
    xf = x.astype(jnp.float32)
    y = xf * lax.rsqrt(jnp.mean(xf * xf, axis=-1, keepdims=True) + NORM_EPS)
    return (y * w.astype(jnp.float32)).astype(x.dtype)


def split_heads(t, n_heads):
    b, l, _ = t.shape
    return t.reshape(b, l, n_heads, -1).transpose(0, 2, 1, 3)


def merge_heads(t):
    b, h, l, d = t.shape
    return t.transpose(0, 2, 1, 3).reshape(b, l, h * d)


def rotary(x, pos):
    half = x.shape[-1] // 2
    inv_freq = ROPE_THETA ** (-jnp.arange(half, dtype=jnp.float32) / half)
    ang = pos.astype(jnp.float32)[:, None] * inv_freq[None, :]
    cos, sin = jnp.cos(ang), jnp.sin(ang)
    x1, x2 = x[..., :half], x[..., half:]
    return jnp.concatenate([x1 * cos - x2 * sin, x1 * sin + x2 * cos], axis=-1)


def hgrn2_mixer(q, f_logit, i_val, g, lb, onorm_w):
    bsz, L, _ = q.shape
    nc = L // A_CHUNK
    f = lb + (1.0 - lb) * jax.nn.sigmoid(f_logit)
    log_f = jnp.log(f)
    k = 1.0 - f

    def chunked(t):
        return split_heads(t, A_HEADS).reshape(bsz, A_HEADS, nc, A_CHUNK, -1)

    qc, kc, vc, lfc = chunked(q), chunked(k), chunked(i_val), chunked(log_f)
    b = jnp.cumsum(lfc, axis=3)
    b_last = b[:, :, :, -1:, :]
    q_dec = qc * jnp.exp(b)
    k_inv = kc * jnp.exp(-b)
    k_end = kc * jnp.exp(b_last - b)
    decay = jnp.exp(b_last[:, :, :, 0, :])
    causal = jnp.tril(jnp.ones((A_CHUNK, A_CHUNK), dtype=bool))
    scores = jnp.einsum('bhncd,bhnsd->bhncs', q_dec, k_inv)
    scores = jnp.where(causal, scores, 0.0)
    o_intra = jnp.einsum('bhncs,bhnsv->bhncv', scores, vc)

    def step(S, inp):
        qd, ke, v, dec = inp
        o = jnp.einsum('bhcd,bhdv->bhcv', qd, S)
        S = dec[..., None] * S + jnp.einsum('bhcd,bhcv->bhdv', ke, v)
        return S, o

    S0 = jnp.zeros((bsz, A_HEADS, A_DK, A_DV), jnp.float32)
    xs = (jnp.moveaxis(q_dec, 2, 0), jnp.moveaxis(k_end, 2, 0),
          jnp.moveaxis(vc, 2, 0), jnp.moveaxis(decay, 2, 0))
    _, o_inter = lax.scan(step, S0, xs)
    o = (o_intra + jnp.moveaxis(o_inter, 0, 2)).reshape(bsz, A_HEADS, L, A_DV)
    o = o * lax.rsqrt(jnp.mean(o * o, axis=-1, keepdims=True) + NORM_EPS)
    o = merge_heads(o) * onorm_w.astype(jnp.float32)
    return o * jax.nn.silu(g)


def dilated_branch(q, k, v, dil, span):
    bsz, H, L, hd = q.shape
    Ls = L // dil
    nb = -(-Ls // span)
    pad = nb * span - Ls

    def strided(t):
        t = t.reshape(bsz, H, Ls, dil, hd).transpose(0, 1, 3, 2, 4)
        t = jnp.pad(t, ((0, 0), (0, 0), (0, 0), (0, pad), (0, 0)))
        return t.reshape(bsz, H, dil, nb, span, hd)

    def with_prev(t):
        prev = jnp.pad(t[:, :, :, :-1], ((0, 0), (0, 0), (0, 0), (1, 0), (0, 0), (0, 0)))
        return jnp.concatenate([prev, t], axis=4)

    qb = strided(q)
    kw, vw = with_prev(strided(k)), with_prev(strided(v))
    s = jnp.einsum('bhrnqd,bhrnkd->bhrnqk', qb, kw) * (hd ** -0.5)
    iq = jnp.arange(span)[:, None]
    ik = jnp.arange(2 * span)[None, :]
    delta = span + iq - ik
    kpos = (jnp.arange(nb)[:, None, None] - 1) * span + ik[None]
    mask = (delta >= 0) & (delta <= span) & (kpos >= 0)
    s = jnp.where(mask, s, -jnp.inf)
    m = jnp.max(s, axis=-1, keepdims=True)
    p = jnp.exp(s - m)
    l = jnp.sum(p, axis=-1)
    o = jnp.einsum('bhrnqk,bhrnkd->bhrnqd', p, vw) / l[..., None]
    lse = m[..., 0] + jnp.log(l)

    def unstride(t):
        t = t.reshape(bsz, H, dil, nb * span, -1)[:, :, :, :Ls]
        return t.transpose(0, 1, 3, 2, 4).reshape(bsz, H, L, -1)

    return unstride(o), unstride(lse[..., None])[..., 0]


def dilated_attention(q, k, v):
    outs, lses = [], []
    for window, dil in B_DILATIONS:
        o, lse = dilated_branch(q, k, v, dil, window // dil)
        outs.append(o)
        lses.append(lse)
    wts = jax.nn.softmax(jnp.stack(lses), axis=0)
    return jnp.einsum('gbhl,gbhld->bhld', wts, jnp.stack(outs))


def mix_ab(h, w_in, w_out, lb, onorm_w, pos):
    z = (h @ w_in).astype(jnp.float32)
    offs = np.cumsum([A_WIDTH] * 4 + [B_WIDTH] * 2).tolist()
    qa, fa, ia, ga, qb, kb, vb = jnp.split(z, offs, axis=-1)
    oa = hgrn2_mixer(qa, fa, ia, ga, lb, onorm_w)
    qh = rotary(split_heads(qb, B_HEADS), pos)
    kh = rotary(split_heads(kb, B_HEADS), pos)
    vh = split_heads(vb, B_HEADS)
    ob = merge_heads(dilated_attention(qh, kh, vh))
    y = jnp.concatenate([oa, ob], axis=-1).astype(h.dtype)
    return y @ w_out


def complex_affine(e1, e2):
    a1r, a1i, b1r, b1i = e1
    a2r, a2i, b2r, b2i = e2
    return (a2r * a1r - a2i * a1i,
            a2r * a1i + a2i * a1r,
            a2r * b1r - a2i * b1i + b2r,
            a2r * b1i + a2i * b1r + b2i)


def s5_mixer(h, lam_re, lam_im, log_dt, b_re, b_im, c_re, c_im, d_skip, w_glu):
    f32 = jnp.float32
    u = h.astype(f32)
    bsz, L, _ = u.shape
    lr = jnp.minimum(lam_re.astype(f32), C_MIN_NEG_RE)
    li = lam_im.astype(f32)
    dt = jnp.exp(log_dt.astype(f32))[:, None]
    mag = jnp.exp(dt * lr)
    ar, ai = mag * jnp.cos(dt * li), mag * jnp.sin(dt * li)
    den = lr * lr + li * li
    zr = ((ar - 1.0) * lr + ai * li) / den
    zi = (ai * lr - (ar - 1.0) * li) / den
    br, bi = b_re.astype(f32), b_im.astype(f32)
    bbr = zr[..., None] * br - zi[..., None] * bi
    bbi = zr[..., None] * bi + zi[..., None] * br
    cr, ci = c_re.astype(f32), c_im.astype(f32)
    nc = L // C_CHUNK
    ug = u.reshape(bsz, nc, C_CHUNK, C_GROUPS, C_GROUP).transpose(1, 0, 2, 3, 4)

    def step(state, uc):
        hr, hi = state
        xr = jnp.einsum('btgc,gpc->btgp', uc, bbr)
        xi = jnp.einsum('btgc,gpc->btgp', uc, bbi)
        a_r = jnp.broadcast_to(ar, xr.shape)
        a_i = jnp.broadcast_to(ai, xr.shape)
        pr, pi_, sr, si = lax.associative_scan(complex_affine, (a_r, a_i, xr, xi), axis=1)
        str_ = pr * hr[:, None] - pi_ * hi[:, None] + sr
        sti = pr * hi[:, None] + pi_ * hr[:, None] + si
        y = (jnp.einsum('btgp,gcp->btgc', str_, cr)
             - jnp.einsum('btgp,gcp->btgc', sti, ci))
        return (str_[:, -1], sti[:, -1]), y

    h0 = (jnp.zeros((bsz, C_GROUPS, C_STATE), f32), jnp.zeros((bsz, C_GROUPS, C_STATE), f32))
    _, y = lax.scan(step, h0, ug)
    y = y.transpose(1, 0, 2, 3, 4).reshape(bsz, L, D_MODEL)
    y = y + d_skip.astype(f32) * u
    gl = jax.nn.gelu(y, approximate=False).astype(h.dtype)
    z = gl @ w_glu
    return z[..., :D_MODEL] * jax.nn.sigmoid(z[..., D_MODEL:])


def memory_cross_attention(h, mem_n, wq, wkv, wo):
    q = split_heads(h @ wq, X_HEADS).astype(jnp.float32)
    kv = mem_n @ wkv
    k = split_heads(kv[..., :D_MODEL], X_HEADS).astype(jnp.float32)
    v = split_heads(kv[..., D_MODEL:], X_HEADS).astype(jnp.float32)
    s = jnp.einsum('bhqd,bhkd->bhqk', q, k) * (X_HD ** -0.5)
    p = jax.nn.softmax(s, axis=-1)
    o = jnp.einsum('bhqk,bhkd->bhqd', p, v).astype(h.dtype)
    return merge_heads(o) @ wo


def swiglu(h, w_in, w_out):
    z = h @ w_in
    return (jax.nn.silu(z[..., :D_FF]) * z[..., D_FF:]) @ w_out


def setup_inputs(seed: int = 0) -> dict:
    key = jax.random.key(seed)
    ks = jax.random.split(key, 24)
    f32 = jnp.float32
    nrm = lambda k, shape, scale: jax.random.normal(k, shape, f32) * scale
    return {
        "x": nrm(ks[0], (BATCH, SEQ, D_MODEL), 1.0),
        "mem": nrm(ks[1], (BATCH, MEM_LEN, D_MODEL), 1.0),
        "norm_w": 1.0 + nrm(ks[2], (DEPTH, 6, D_MODEL), 0.05),
        "mem_norm_w": 1.0 + nrm(ks[3], (DEPTH, D_MODEL), 0.05),
        "ab_w_in": nrm(ks[4], (N_EVEN, D_MODEL, IN_AB_WIDTH), D_MODEL ** -0.5),
        "ab_w_out": nrm(ks[5], (N_EVEN, MIX_WIDTH, D_MODEL), MIX_WIDTH ** -0.5),
        "hgrn_lb_logits": nrm(ks[6], (DEPTH + 1, A_WIDTH), 0.1),
        "hgrn_out_norm_w": 1.0 + nrm(ks[7], (N_EVEN, A_WIDTH), 0.05),
        "s5_lambda_re": -0.5 + nrm(ks[8], (N_ODD, C_GROUPS, C_STATE), 0.01),
        "s5_lambda_im": np.pi * jnp.arange(C_STATE, dtype=f32) + nrm(ks[9], (N_ODD, C_GROUPS, C_STATE), 0.01),
        "s5_log_dt": jax.random.uniform(ks[10], (N_ODD, C_GROUPS), f32, math.log(1e-3), math.log(1e-1)),
        "s5_b_re": nrm(ks[11], (N_ODD, C_GROUPS, C_STATE, C_GROUP), (2 * C_GROUP) ** -0.5),
        "s5_b_im": nrm(ks[12], (N_ODD, C_GROUPS, C_STATE, C_GROUP), (2 * C_GROUP) ** -0.5),
        "s5_c_re": nrm(ks[13], (N_ODD, C_GROUPS, C_GROUP, C_STATE), 2.0 ** -0.5),
        "s5_c_im": nrm(ks[14], (N_ODD, C_GROUPS, C_GROUP, C_STATE), 2.0 ** -0.5),
        "s5_d": nrm(ks[15], (N_ODD, D_MODEL), 1.0),
        "s5_w_glu": nrm(ks[16], (N_ODD, D_MODEL, 2 * D_MODEL), D_MODEL ** -0.5),
        "xattn_wq": nrm(ks[17], (DEPTH, D_MODEL, D_MODEL), D_MODEL ** -0.5),
        "xattn_wkv": nrm(ks[18], (DEPTH, D_MODEL, 2 * D_MODEL), D_MODEL ** -0.5),
        "xattn_wo": nrm(ks[19], (DEPTH, D_MODEL, D_MODEL), D_MODEL ** -0.5),
        "ffn_w_in": nrm(ks[20], (DEPTH, D_MODEL, 2 * D_FF), D_MODEL ** -0.5),
        "ffn_w_out": nrm(ks[21], (DEPTH, D_FF, D_MODEL), D_FF ** -0.5),
    }


def reference(x, mem, norm_w, mem_norm_w, ab_w_in, ab_w_out, hgrn_lb_logits,
              hgrn_out_norm_w, s5_lambda_re, s5_lambda_im, s5_log_dt, s5_b_re,
              s5_b_im, s5_c_re, s5_c_im, s5_d, s5_w_glu, xattn_wq, xattn_wkv,
              xattn_wo, ffn_w_in, ffn_w_out):
    L = x.shape[1]
    pos = jnp.arange(L, dtype=jnp.int32)
    lb_table = jnp.cumsum(jax.nn.softmax(hgrn_lb_logits.astype(jnp.float32), axis=0), axis=0)
    for layer in range(DEPTH):
        j = layer // 2
        h = rms_norm(x, norm_w[layer, 0])
        if layer % 2 == 0:
            y = mix_ab(h, ab_w_in[j], ab_w_out[j], lb_table[layer], hgrn_out_norm_w[j], pos)
        else:
            y = s5_mixer(h, s5_lambda_re[j], s5_lambda_im[j], s5_log_dt[j], s5_b_re[j],
                         s5_b_im[j], s5_c_re[j], s5_c_im[j], s5_d[j], s5_w_glu[j])
        x = x + rms_norm(y, norm_w[layer, 1])
        h = rms_norm(x, norm_w[layer, 2])
        mem_n = rms_norm(mem, mem_norm_w[layer])
        y = memory_cross_attention(h, mem_n, xattn_wq[layer], xattn_wkv[layer], xattn_wo[layer])
        x = x + rms_norm(y, norm_w[layer, 3])
        h = rms_norm(x, norm_w[layer, 4])
        x = x + rms_norm(swiglu(h, ffn_w_in[layer], ffn_w_out[layer]), norm_w[layer, 5])
    return x
```

```python
import functools
import math

import jax
import jax.numpy as jnp
import numpy as np
from jax import lax
from jax.experimental import pallas as pl
from jax.experimental.pallas import tpu as pltpu

F32 = jnp.float32
BF16 = jnp.bfloat16

D_MODEL = 1024
NORM_EPS = 1e-6
A_HEADS = 4
A_DK = 128
A_WIDTH = 512
A_CHUNK = 32
B_HEADS = 4
B_HD = 128
B_WIDTH = 512
B_SPAN = 128
B_DILS = (1, 4, 16)
ROPE_THETA = 10000.0
IN_AB_WIDTH = 4 * A_WIDTH + 3 * B_WIDTH
C_GROUP = 16
C_GROUPS = 64
C_STATE = 64
C_MIN_NEG_RE = -1e-4
S5_T = 16
MEM_LEN = 256
X_HEADS = 4
X_HD = 256
D_FF = 2816

TM = 512
FF_CHUNK = 1408
VMEM_LIMIT = 56 * 1024 * 1024

NT_DIMS = (((1,), (1,)), ((), ()))
TN_DIMS = (((0,), (0,)), ((), ()))


def _cparams(n_axes):
    return pltpu.CompilerParams(
        dimension_semantics=("arbitrary",) * n_axes, vmem_limit_bytes=VMEM_LIMIT)


def _resident(shape):
    zeros = (0,) * len(shape)
    return pl.BlockSpec(shape, lambda *_: zeros, pipeline_mode=pl.Buffered(1))


def _rms(xf, w):
    return xf * lax.rsqrt(jnp.mean(xf * xf, axis=-1, keepdims=True) + NORM_EPS) * w


def _dot(a, b):
    return jnp.dot(a, b, preferred_element_type=F32)


def _proj_ab_kernel(x_ref, nw_ref, w_ref, cos_ref, sin_ref,
                    qa_ref, fa_ref, ia_ref, ga_ref, qb_ref, kb_ref, vb_ref):
    h = _rms(x_ref[...], nw_ref[...]).astype(BF16)
    cos = cos_ref[...]
    sin = sin_ref[...]
    outs = (qa_ref, fa_ref, ia_ref, ga_ref, qb_ref, kb_ref, vb_ref)
    for j, o_ref in enumerate(outs):
        z = _dot(h, w_ref[:, j * 512:(j + 1) * 512])
        for hd in range(4):
            zh = z[:, hd * 128:(hd + 1) * 128]
            if j in (4, 5):
                zh = zh * cos + pltpu.roll(zh, 64, 1) * sin
            if j == 4:
                zh = zh * (B_HD ** -0.5)
            o_ref[hd] = zh.astype(o_ref.dtype)


def _proj_ab(x2, nw, w_in, cos_t, sin_t):
    n = x2.shape[0]
    pos_blocks = cos_t.shape[0] // TM
    head_out = lambda dt: jax.ShapeDtypeStruct((4, n, 128), dt)
    head_spec = pl.BlockSpec((4, TM, 128), lambda i: (0, i, 0))
    return pl.pallas_call(
        _proj_ab_kernel,
        grid=(n // TM,),
        in_specs=[
            pl.BlockSpec((TM, D_MODEL), lambda i: (i, 0)),
            _resident((1, D_MODEL)),
            _resident((D_MODEL, IN_AB_WIDTH)),
            pl.BlockSpec((TM, 128), lambda i: (i % pos_blocks, 0)),
            pl.BlockSpec((TM, 128), lambda i: (i % pos_blocks, 0)),
        ],
        out_specs=[head_spec] * 7,
        out_shape=[head_out(BF16), head_out(F32), head_out(BF16), head_out(BF16),
                   head_out(BF16), head_out(BF16), head_out(BF16)],
        compiler_params=_cparams(1),
        name="proj_ab",
    )(x2, nw, w_in, cos_t, sin_t)


HG_BLK = 256


def _hgrn_kernel(q_ref, f_ref, i_ref, g_ref, lb_ref, onw_ref, o_ref,
                 qdec_s, kend_s, dec_s, oacc_s):
    seq = q_ref.shape[0]
    lb = lb_ref[...]
    ri = lax.broadcasted_iota(jnp.int32, (2 * HG_BLK, HG_BLK), 0)
    ci = lax.broadcasted_iota(jnp.int32, (2 * HG_BLK, HG_BLK), 1)
    rr = jnp.where(ri >= HG_BLK, ri - HG_BLK, ri)
    same = (rr // A_CHUNK) == (ci // A_CHUNK)
    lmat = jnp.where(same & ((ci <= rr) | (ri >= HG_BLK)), 1.0, 0.0).astype(BF16)
    causal = (same & (ci <= rr))[:HG_BLK]

    for blk in range(seq // HG_BLK):
        rows = pl.ds(blk * HG_BLK, HG_BLK)
        f = lb + (1.0 - lb) * jax.nn.sigmoid(f_ref[rows, :])
        log_f = jnp.log(f)
        k = 1.0 - f
        hi = log_f.astype(BF16)
        lo = (log_f - hi.astype(F32)).astype(BF16)
        bb = _dot(lmat, hi) + _dot(lmat, lo)
        b = bb[:HG_BLK]
        b_last = bb[HG_BLK:]
        q_dec = (q_ref[rows, :].astype(F32) * jnp.exp(b)).astype(BF16)
        k_inv = (k * jnp.exp(-b)).astype(BF16)
        qdec_s[rows, :] = q_dec
        kend_s[rows, :] = (k * jnp.exp(b_last - b)).astype(BF16)
        dec_s[rows, :] = jnp.exp(b_last)
        scores = lax.dot_general(q_dec, k_inv, NT_DIMS, preferred_element_type=F32)
        scores = jnp.where(causal, scores, 0.0).astype(BF16)
        oacc_s[rows, :] = _dot(scores, i_ref[rows, :])

    def chunk_step(n, state_t):
        r = pl.multiple_of(n * A_CHUNK, A_CHUNK)
        rows = pl.ds(r, A_CHUNK)
        qd = qdec_s[rows, :]
        o = lax.dot_general(qd, state_t.astype(BF16), NT_DIMS, preferred_element_type=F32)
        oacc_s[rows, :] += o
        upd = lax.dot_general(i_ref[rows, :], kend_s[rows, :], TN_DIMS,
                              preferred_element_type=F32)
        return dec_s[pl.ds(r, 1), :] * state_t + upd

    lax.fori_loop(0, seq // A_CHUNK, chunk_step, jnp.zeros((A_DK, A_DK), F32))

    o = oacc_s[...]
    o = o * lax.rsqrt(jnp.mean(o * o, axis=-1, keepdims=True) + NORM_EPS) * onw_ref[...]
    g = g_ref[...].astype(F32)
    o_ref[...] = (o * (g * jax.nn.sigmoid(g))).astype(o_ref.dtype)


def _hgrn(qa, fa, ia, ga, lb, onw, bsz, seq):
    view = lambda t: t.reshape(4, bsz, seq, 128)
    head_spec = pl.BlockSpec((None, None, seq, 128), lambda b, h: (h, b, 0, 0))
    vec_spec = pl.BlockSpec((None, 1, 128), lambda b, h: (h, 0, 0))
    return pl.pallas_call(
        _hgrn_kernel,
        grid=(bsz, A_HEADS),
        in_specs=[head_spec] * 4 + [vec_spec] * 2,
        out_specs=pl.BlockSpec((None, seq, 128), lambda b, h: (b, 0, h)),
        out_shape=jax.ShapeDtypeStruct((bsz, seq, A_WIDTH), BF16),
        scratch_shapes=[pltpu.VMEM((seq, 128), BF16), pltpu.VMEM((seq, 128), BF16),
                        pltpu.VMEM((seq, 128), F32), pltpu.VMEM((seq, 128), F32)],
        compiler_params=_cparams(2),
        name="hgrn2",
    )(view(qa), view(fa), view(ia), view(ga), lb.reshape(4, 1, 128), onw.reshape(4, 1, 128))


def _dil_kernel(q_ref, k_ref, v_ref, o_ref, q_s, k_s, v_s, acc_s, m_s, l_s):
    q_s[...] = q_ref[...].astype(F32)
    k_s[...] = k_ref[...].astype(F32)
    v_s[...] = v_ref[...].astype(F32)

    iq = lax.broadcasted_iota(jnp.int32, (B_SPAN, 2 * B_SPAN), 0)
    ik = lax.broadcasted_iota(jnp.int32, (B_SPAN, 2 * B_SPAN), 1)
    mask_first = (ik <= iq)[:, :B_SPAN]
    mask_win = (ik >= iq) & (ik <= iq + B_SPAN)

    def attend(q_rows, kv_rows, mask, first):
        q = q_s[q_rows, :].astype(BF16)
        kk = k_s[kv_rows, :].astype(BF16)
        vv = v_s[kv_rows, :].astype(BF16)
        s = lax.dot_general(q, kk, NT_DIMS, preferred_element_type=F32)
        s = jnp.where(mask, s, -jnp.inf)
        m_blk = jnp.max(s, axis=-1, keepdims=True)
        if first:
            m_new = jnp.broadcast_to(m_blk, (B_SPAN, 128))
        else:
            m_old = m_s[q_rows, :]
            m_new = jnp.maximum(m_old, m_blk)
        reps = s.shape[1] // 128
        p = jnp.exp(s - jnp.concatenate([m_new] * reps, axis=1))
        l_blk = jnp.sum(p, axis=-1, keepdims=True)
        pv = _dot(p.astype(BF16), vv)
        if first:
            l_s[q_rows, :] = jnp.broadcast_to(l_blk, (B_SPAN, 128))
            acc_s[q_rows, :] = pv
        else:
            alpha = jnp.exp(m_old - m_new)
            l_s[q_rows, :] = alpha * l_s[q_rows, :] + l_blk
            acc_s[q_rows, :] = alpha * acc_s[q_rows, :] + pv
        m_s[q_rows, :] = m_new

    for dil in B_DILS:
        first = dil == 1
        blocks = q_ref.shape[0] // (dil * B_SPAN)
        stride = None if dil == 1 else dil

        def head_block(r, _, dil=dil, first=first, stride=stride):
            attend(pl.ds(r, B_SPAN, stride=stride), pl.ds(r, B_SPAN, stride=stride),
                   mask_first, first)
            return 0

        def window_block(idx, _, dil=dil, first=first, stride=stride, blocks=blocks):
            r = idx // (blocks - 1)
            n = idx % (blocks - 1) + 1
            q0 = r + n * (dil * B_SPAN)
            k0 = q0 - dil * B_SPAN
            if dil == 1:
                q0 = pl.multiple_of(q0, B_SPAN)
                k0 = pl.multiple_of(k0, B_SPAN)
            attend(pl.ds(q0, B_SPAN, stride=stride), pl.ds(k0, 2 * B_SPAN, stride=stride),
                   mask_win, first)
            return 0

        lax.fori_loop(0, dil, head_block, 0)
        if blocks > 1:
            lax.fori_loop(0, dil * (blocks - 1), window_block, 0)

    o_ref[...] = (acc_s[...] / l_s[...]).astype(o_ref.dtype)


def _dil_attn(qb, kb, vb, bsz, seq):
    view = lambda t: t.reshape(4, bsz, seq, 128)
    head_spec = pl.BlockSpec((None, None, seq, 128), lambda b, h: (h, b, 0, 0))
    return pl.pallas_call(
        _dil_kernel,
        grid=(bsz, B_HEADS),
        in_specs=[head_spec] * 3,
        out_specs=pl.BlockSpec((None, seq, 128), lambda b, h: (b, 0, h)),
        out_shape=jax.ShapeDtypeStruct((bsz, seq, B_WIDTH), BF16),
        scratch_shapes=[pltpu.VMEM((seq, 128), F32)] * 6,
        compiler_params=_cparams(2),
        name="dilated_attn",
    )(view(qb), view(kb), view(vb))


def _out_proj_kernel(oa_ref, ob_ref, w_ref, nw_ref, x_ref, o_ref):
    y = _dot(oa_ref[...], w_ref[:A_WIDTH, :]) + _dot(ob_ref[...], w_ref[A_WIDTH:, :])
    o_ref[...] = x_ref[...] + _rms(y, nw_ref[...])


def _out_proj(oa, ob, w_out, nw, x2):
    n = x2.shape[0]
    row = lambda w: pl.BlockSpec((TM, w), lambda i: (i, 0))
    return pl.pallas_call(
        _out_proj_kernel,
        grid=(n // TM,),
        in_specs=[row(A_WIDTH), row(B_WIDTH), _resident((D_MODEL, D_MODEL)),
                  _resident((1, D_MODEL)), row(D_MODEL)],
        out_specs=row(D_MODEL),
        out_shape=jax.ShapeDtypeStruct((n, D_MODEL), F32),
        compiler_params=_cparams(1),
        name="out_proj",
    )(oa, ob, w_out, nw, x2)


def _kv_kernel(mem_ref, nw_ref, w_ref, k_ref, v_ref):
    h = _rms(mem_ref[...], nw_ref[...]).astype(BF16)
    k_ref[...] = _dot(h, w_ref[:, :D_MODEL]).astype(BF16)
    v_ref[...] = _dot(h, w_ref[:, D_MODEL:]).astype(BF16)


def _kv_proj(mem, nw, wkv):
    bsz = mem.shape[0]
    blk = pl.BlockSpec((None, MEM_LEN, D_MODEL), lambda b: (b, 0, 0))
    kv = jax.ShapeDtypeStruct((bsz, MEM_LEN, D_MODEL), BF16)
    return pl.pallas_call(
        _kv_kernel,
        grid=(bsz,),
        in_specs=[blk, _resident((1, D_MODEL)), _resident((D_MODEL, 2 * D_MODEL))],
        out_specs=[blk, blk],
        out_shape=[kv, kv],
        compiler_params=_cparams(1),
        name="xattn_kv",
    )(mem, nw, wkv)


def _xattn_kernel(x_ref, k_ref, v_ref, wq_ref, wo_ref, nwa_ref, nwb_ref, o_ref):
    x = x_ref[...]
    h = _rms(x, nwa_ref[...]).astype(BF16)
    q = _dot(h, wq_ref[...])
    heads = []
    for hd in range(X_HEADS):
        cols = slice(hd * X_HD, (hd + 1) * X_HD)
        qh = (q[:, cols] * (X_HD ** -0.5)).astype(BF16)
        s = lax.dot_general(qh, k_ref[:, cols], NT_DIMS, preferred_element_type=F32)
        p = jnp.exp(s - jnp.max(s, axis=-1, keepdims=True))
        l = jnp.sum(p, axis=-1, keepdims=True)
        heads.append((_dot(p.astype(BF16), v_ref[:, cols]) / l).astype(BF16))
    y = _dot(jnp.concatenate(heads, axis=1), wo_ref[...])
    o_ref[...] = x + _rms(y, nwb_ref[...])


def _xattn(x3, k, v, wq, wo, nwa, nwb):
    bsz, seq, _ = x3.shape
    row = pl.BlockSpec((None, TM, D_MODEL), lambda b, i: (b, i, 0))
    kvs = pl.BlockSpec((None, MEM_LEN, D_MODEL), lambda b, i: (b, 0, 0))
    return pl.pallas_call(
        _xattn_kernel,
        grid=(bsz, seq // TM),
        in_specs=[row, kvs, kvs, _resident((D_MODEL, D_MODEL)), _resident((D_MODEL, D_MODEL)),
                  _resident((1, D_MODEL)), _resident((1, D_MODEL))],
        out_specs=row,
        out_shape=jax.ShapeDtypeStruct(x3.shape, F32),
        compiler_params=_cparams(2),
        name="xattn",
    )(x3, k, v, wq, wo, nwa, nwb)


def _ffn_kernel(x_ref, nwa_ref, w_in_ref, w_out_ref, nwb_ref, o_ref):
    x = x_ref[...]
    h = _rms(x, nwa_ref[...]).astype(BF16)
    y = None
    for j in range(D_FF // FF_CHUNK):
        z1 = _dot(h, w_in_ref[:, j * FF_CHUNK:(j + 1) * FF_CHUNK])
        z2 = _dot(h, w_in_ref[:, D_FF + j * FF_CHUNK:D_FF + (j + 1) * FF_CHUNK])
        a = (z1 * jax.nn.sigmoid(z1) * z2).astype(BF16)
        part = _dot(a, w_out_ref[j * FF_CHUNK:(j + 1) * FF_CHUNK, :])
        y = part if y is None else y + part
    o_ref[...] = x + _rms(y, nwb_ref[...])


def _ffn(x2, nwa, w_in, w_out, nwb):
    n = x2.shape[0]
    row = pl.BlockSpec((TM, D_MODEL), lambda i: (i, 0))
    return pl.pallas_call(
        _ffn_kernel,
        grid=(n // TM,),
        in_specs=[row, _resident((1, D_MODEL)), _resident((D_MODEL, 2 * D_FF)),
                  _resident((D_FF, D_MODEL)), _resident((1, D_MODEL))],
        out_specs=row,
        out_shape=jax.ShapeDtypeStruct((n, D_MODEL), F32),
        compiler_params=_cparams(1),
        name="ffn",
    )(x2, nwa, w_in, w_out, nwb)


def _norm_cast_kernel(x_ref, nw_ref, o_ref):
    o_ref[...] = _rms(x_ref[...], nw_ref[...]).astype(o_ref.dtype)


def _norm_cast(x2, nw):
    n = x2.shape[0]
    row = pl.BlockSpec((TM, D_MODEL), lambda i: (i, 0))
    return pl.pallas_call(
        _norm_cast_kernel,
        grid=(n // TM,),
        in_specs=[row, _resident((1, D_MODEL))],
        out_specs=row,
        out_shape=jax.ShapeDtypeStruct((n, D_MODEL), BF16),
        compiler_params=_cparams(1),
        name="norm_cast",
    )(x2, nw)


def _s5_kernel(u_ref, w1_ref, p_ref, a_ref, y_ref, dh_s, hprev_s, *, bsz):
    feat = S5_T * C_GROUP
    r1 = _dot(u_ref[...], w1_ref[...])
    y_ref[...] = r1[:, :feat]
    dh_s[...] = r1[:, feat:]
    a1 = a_ref[0:1, :]
    a2 = a_ref[1:2, :]
    a2s = a_ref[2:3, :]
    n_chunks = u_ref.shape[0] // bsz

    def step(n, carry):
        h, hs = carry
        rows = pl.ds(pl.multiple_of(n * bsz, bsz), bsz)
        hprev_s[rows, :] = h
        d = dh_s[rows, :]
        return (a1 * h + a2 * hs + d[:, :2 * C_STATE],
                a1 * hs + a2s * h + d[:, 2 * C_STATE:])

    zero = jnp.zeros((bsz, 2 * C_STATE), F32)
    lax.fori_loop(0, n_chunks, step, (zero, zero))
    y_ref[...] += _dot(hprev_s[...].astype(BF16), p_ref[...])


def _s5_scan(u_g, w1, pmat, avec, bsz):
    groups, rows, feat = u_g.shape
    return pl.pallas_call(
        functools.partial(_s5_kernel, bsz=bsz),
        grid=(groups,),
        in_specs=[pl.BlockSpec((None, rows, feat), lambda g: (g, 0, 0)),
                  pl.BlockSpec((None, feat, feat + 4 * C_STATE), lambda g: (g, 0, 0)),
                  pl.BlockSpec((None, 2 * C_STATE, feat), lambda g: (g, 0, 0)),
                  pl.BlockSpec((None, 8, 2 * C_STATE), lambda g: (g, 0, 0))],
        out_specs=pl.BlockSpec((None, rows, feat), lambda g: (g, 0, 0)),
        out_shape=jax.ShapeDtypeStruct((groups, rows, feat), F32),
        scratch_shapes=[pltpu.VMEM((rows, 4 * C_STATE), F32),
                        pltpu.VMEM((rows, 2 * C_STATE), F32)],
        compiler_params=_cparams(1),
        name="s5_scan",
    )(u_g, w1, pmat, avec)


def _s5_weights(lam_re, lam_im, log_dt, b_re, b_im, c_re, c_im):
    hp = lax.Precision.HIGHEST
    lr = jnp.minimum(lam_re.astype(F32), C_MIN_NEG_RE)
    li = lam_im.astype(F32)
    dt = jnp.exp(log_dt.astype(F32))[:, None]
    mag = jnp.exp(dt * lr)
    ar, ai = mag * jnp.cos(dt * li), mag * jnp.sin(dt * li)
    den = lr * lr + li * li
    zr = ((ar - 1.0) * lr + ai * li) / den
    zi = (ai * lr - (ar - 1.0) * li) / den
    br, bi = b_re.astype(F32), b_im.astype(F32)
    bbr = zr[..., None] * br - zi[..., None] * bi
    bbi = zr[..., None] * bi + zi[..., None] * br
    cr, ci = c_re.astype(F32), c_im.astype(F32)
    kk = jnp.arange(S5_T + 1, dtype=F32)[:, None, None]
    pmag = jnp.exp(kk * (dt * lr))
    pr, pi_ = pmag * jnp.cos(kk * (dt * li)), pmag * jnp.sin(kk * (dt * li))
    car = cr[None] * pr[:, :, None, :] - ci[None] * pi_[:, :, None, :]
    cai = cr[None] * pi_[:, :, None, :] + ci[None] * pr[:, :, None, :]
    ker = (jnp.einsum('kgcp,gpd->kgcd', car, bbr, precision=hp)
           - jnp.einsum('kgcp,gpd->kgcd', cai, bbi, precision=hp))
    s_idx = jnp.arange(S5_T)[:, None]
    t_idx = jnp.arange(S5_T)[None, :]
    lag = t_idx - s_idx
    toe = jnp.where((lag >= 0)[:, :, None, None, None], ker[jnp.clip(lag, 0)], 0.0)
    feat = S5_T * C_GROUP
    m_mat = toe.transpose(2, 0, 4, 1, 3).reshape(C_GROUPS, feat, feat)
    rev = S5_T - 1 - jnp.arange(S5_T)
    qr = pr[rev][..., None] * bbr[None] - pi_[rev][..., None] * bbi[None]
    qi = pr[rev][..., None] * bbi[None] + pi_[rev][..., None] * bbr[None]
    to_rows = lambda t: t.transpose(1, 0, 3, 2).reshape(C_GROUPS, feat, C_STATE)
    qr, qi = to_rows(qr), to_rows(qi)
    w1 = jnp.concatenate([m_mat, qr, qi, qi, qr], axis=-1)
    to_cols = lambda t: t.transpose(1, 3, 0, 2).reshape(C_GROUPS, C_STATE, feat)
    pmat = jnp.concatenate([to_cols(car[1:]), -to_cols(cai[1:])], axis=1)
    atr, ati = pr[S5_T], pi_[S5_T]
    zeros = jnp.zeros_like(atr)
    avec = jnp.stack([jnp.concatenate([atr, atr], -1), jnp.concatenate([-ati, ati], -1),
                      jnp.concatenate([ati, -ati], -1)] + [jnp.concatenate([zeros, zeros], -1)] * 5,
                     axis=1)
    return w1.astype(BF16), pmat.astype(BF16), avec


def _glu_kernel(y_ref, x_ref, nwa_ref, d_ref, w_ref, nwb_ref, o_ref):
    x = x_ref[...]
    u = _rms(x, nwa_ref[...])
    t = y_ref[...] + d_ref[...] * u
    gl = (0.5 * t * (1.0 + lax.erf(t * math.sqrt(0.5)))).astype(BF16)
    z = _dot(gl, w_ref[...])
    y = z[:, :D_MODEL] * jax.nn.sigmoid(z[:, D_MODEL:])
    o_ref[...] = x + _rms(y, nwb_ref[...])


def _glu(y2, x2, nwa, d, w_glu, nwb):
    n = x2.shape[0]
    row = pl.BlockSpec((TM, D_MODEL), lambda i: (i, 0))
    return pl.pallas_call(
        _glu_kernel,
        grid=(n // TM,),
        in_specs=[row, row, _resident((1, D_MODEL)), _resident((1, D_MODEL)),
                  _resident((D_MODEL, 2 * D_MODEL)), _resident((1, D_MODEL))],
        out_specs=row,
        out_shape=jax.ShapeDtypeStruct((n, D_MODEL), F32),
        compiler_params=_cparams(1),
        name="s5_glu",
    )(y2, x2, nwa, d, w_glu, nwb)


def _rotary_tables(seq):
    half = B_HD // 2
    inv_freq = ROPE_THETA ** (-jnp.arange(half, dtype=F32) / half)
    ang = jnp.arange(seq, dtype=jnp.int32).astype(F32)[:, None] * inv_freq[None, :]
    cos, sin = jnp.cos(ang), jnp.sin(ang)
    return jnp.concatenate([cos, cos], axis=-1), jnp.concatenate([-sin, sin], axis=-1)


def kernel(x, mem, norm_w, mem_norm_w, ab_w_in, ab_w_out, hgrn_lb_logits, hgrn_out_norm_w, s5_lambda_re, s5_lambda_im, s5_log_dt, s5_b_re, s5_b_im, s5_c_re, s5_c_im, s5_d, s5_w_glu, xattn_wq, xattn_wkv, xattn_wo, ffn_w_in, ffn_w_out):
    bsz, seq, _ = x.shape
    n = bsz * seq
    depth = norm_w.shape[0]
    nw = lambda layer, i: norm_w[layer, i].astype(F32).reshape(1, D_MODEL)
    lb_table = jnp.cumsum(jax.nn.softmax(hgrn_lb_logits.astype(F32), axis=0), axis=0)
    cos_t, sin_t = _rotary_tables(seq)
    x2 = x.reshape(n, D_MODEL)
    for layer in range(depth):
        j = layer // 2
        if layer % 2 == 0:
            qa, fa, ia, ga, qb, kb, vb = _proj_ab(
                x2, nw(layer, 0), ab_w_in[j].astype(BF16), cos_t, sin_t)
            oa = _hgrn(qa, fa, ia, ga, lb_table[layer], hgrn_out_norm_w[j].astype(F32), bsz, seq)
            ob = _dil_attn(qb, kb, vb, bsz, seq)
            x2 = _out_proj(oa.reshape(n, A_WIDTH), ob.reshape(n, B_WIDTH),
                           ab_w_out[j].astype(BF16), nw(layer, 1), x2)
        else:
            w1, pmat, avec = _s5_weights(s5_lambda_re[j], s5_lambda_im[j], s5_log_dt[j],
                                         s5_b_re[j], s5_b_im[j], s5_c_re[j], s5_c_im[j])
            hn = _norm_cast(x2, nw(layer, 0))
            n_chunks = seq // S5_T
            u_g = (hn.reshape(bsz, n_chunks, S5_T, C_GROUPS, C_GROUP)
                   .transpose(3, 1, 0, 2, 4).reshape(C_GROUPS, n_chunks * bsz, S5_T * C_GROUP))
            y_g = _s5_scan(u_g, w1, pmat, avec, bsz)
            y2 = (y_g.reshape(C_GROUPS, n_chunks, bsz, S5_T, C_GROUP)
                  .transpose(2, 1, 3, 0, 4).reshape(n, D_MODEL))
            x2 = _glu(y2, x2, nw(layer, 0), s5_d[j].astype(F32).reshape(1, D_MODEL),
                      s5_w_glu[j].astype(BF16), nw(layer, 1))
        k, v = _kv_proj(mem, mem_norm_w[layer].astype(F32).reshape(1, D_MODEL),
                        xattn_wkv[layer].astype(BF16))
        x2 = _xattn(x2.reshape(bsz, seq, D_MODEL), k, v, xattn_wq[layer].astype(BF16),
                    xattn_wo[layer].astype(BF16), nw(layer, 2), nw(layer, 3)).reshape(n, D_MODEL)
        x2 = _ffn(x2, nw(layer, 4), ffn_w_in[layer].astype(BF16),
                  ffn_w_out[layer].astype(BF16), nw(layer, 5))
    return x2.reshape(bsz, seq, D_MODEL)
```

```python
import functools
import math

import jax
import jax.numpy as jnp
import numpy as np
from jax import lax
from jax.experimental import pallas as pl
from jax.experimental.pallas import tpu as pltpu

F32 = jnp.float32
BF16 = jnp.bfloat16

D_MODEL = 1024
NORM_EPS = 1e-6
A_HEADS = 4
A_DK = 128
A_WIDTH = 512
A_CHUNK = 32
B_HEADS = 4
B_HD = 128
B_WIDTH = 512
B_SPAN = 128
B_DILS = (1, 4, 16)
DIL_UNROLL = 4
ROPE_THETA = 10000.0
IN_AB_WIDTH = 4 * A_WIDTH + 3 * B_WIDTH
C_GROUP = 16
C_GROUPS = 64
C_STATE = 64
C_MIN_NEG_RE = -1e-4
S5_T = 16
MEM_LEN = 256
X_HEADS = 4
X_HD = 256
D_FF = 2816

TM = 512
FF_CHUNK = 1408
VMEM_LIMIT = 56 * 1024 * 1024

NT_DIMS = (((1,), (1,)), ((), ()))
TN_DIMS = (((0,), (0,)), ((), ()))


def _cparams(n_axes):
    return pltpu.CompilerParams(
        dimension_semantics=("arbitrary",) * n_axes, vmem_limit_bytes=VMEM_LIMIT)


def _resident(shape):
    zeros = (0,) * len(shape)
    return pl.BlockSpec(shape, lambda *_: zeros, pipeline_mode=pl.Buffered(1))


def _rms(xf, w):
    return xf * lax.rsqrt(jnp.mean(xf * xf, axis=-1, keepdims=True) + NORM_EPS) * w


def _dot(a, b):
    return jnp.dot(a, b, preferred_element_type=F32)


def _proj_ab_kernel(x_ref, nw_ref, w_ref, cos_ref, sin_ref,
                    qa_ref, fa_ref, ia_ref, ga_ref, qb_ref, kb_ref, vb_ref):
    h = _rms(x_ref[...], nw_ref[...]).astype(BF16)
    cos = cos_ref[...]
    sin = sin_ref[...]
    outs = (qa_ref, fa_ref, ia_ref, ga_ref, qb_ref, kb_ref, vb_ref)
    for j, o_ref in enumerate(outs):
        z = _dot(h, w_ref[:, j * 512:(j + 1) * 512])
        for hd in range(4):
            zh = z[:, hd * 128:(hd + 1) * 128]
            if j in (4, 5):
                zh = zh * cos + pltpu.roll(zh, 64, 1) * sin
            if j == 4:
                zh = zh * (B_HD ** -0.5)
            o_ref[hd] = zh.astype(o_ref.dtype)


def _proj_ab(x2, nw, w_in, cos_t, sin_t):
    n = x2.shape[0]
    pos_blocks = cos_t.shape[0] // TM
    head_out = lambda dt: jax.ShapeDtypeStruct((4, n, 128), dt)
    head_spec = pl.BlockSpec((4, TM, 128), lambda i: (0, i, 0))
    return pl.pallas_call(
        _proj_ab_kernel,
        grid=(n // TM,),
        in_specs=[
            pl.BlockSpec((TM, D_MODEL), lambda i: (i, 0)),
            _resident((1, D_MODEL)),
            _resident((D_MODEL, IN_AB_WIDTH)),
            pl.BlockSpec((TM, 128), lambda i: (i % pos_blocks, 0)),
            pl.BlockSpec((TM, 128), lambda i: (i % pos_blocks, 0)),
        ],
        out_specs=[head_spec] * 7,
        out_shape=[head_out(BF16), head_out(F32), head_out(BF16), head_out(BF16),
                   head_out(BF16), head_out(BF16), head_out(BF16)],
        compiler_params=_cparams(1),
        name="proj_ab",
    )(x2, nw, w_in, cos_t, sin_t)


HG_BLK = 256
HG_UNROLL = 4


def _hgrn_kernel(q_ref, f_ref, i_ref, g_ref, lb_ref, onw_ref, o_ref,
                 qdec_s, kend_s, dec_s, oacc_s):
    seq = q_ref.shape[0]
    lb = lb_ref[...]
    ri = lax.broadcasted_iota(jnp.int32, (2 * HG_BLK, HG_BLK), 0)
    ci = lax.broadcasted_iota(jnp.int32, (2 * HG_BLK, HG_BLK), 1)
    rr = jnp.where(ri >= HG_BLK, ri - HG_BLK, ri)
    same = (rr // A_CHUNK) == (ci // A_CHUNK)
    lmat = jnp.where(same & ((ci <= rr) | (ri >= HG_BLK)), 1.0, 0.0).astype(BF16)
    causal = (same & (ci <= rr))[:HG_BLK]

    for blk in range(seq // HG_BLK):
        rows = pl.ds(blk * HG_BLK, HG_BLK)
        f = lb + (1.0 - lb) * jax.nn.sigmoid(f_ref[rows, :])
        log_f = jnp.log(f)
        k = 1.0 - f
        hi = log_f.astype(BF16)
        lo = (log_f - hi.astype(F32)).astype(BF16)
        bb = _dot(lmat, hi) + _dot(lmat, lo)
        b = bb[:HG_BLK]
        b_last = bb[HG_BLK:]
        q_dec = (q_ref[rows, :].astype(F32) * jnp.exp(b)).astype(BF16)
        k_inv = (k * jnp.exp(-b)).astype(BF16)
        qdec_s[rows, :] = q_dec
        kend_s[rows, :] = (k * jnp.exp(b_last - b)).astype(BF16)
        dec_s[rows, :] = jnp.exp(b_last)
        scores = lax.dot_general(q_dec, k_inv, NT_DIMS, preferred_element_type=F32)
        scores = jnp.where(causal, scores, 0.0).astype(BF16)
        oacc_s[rows, :] = _dot(scores, i_ref[rows, :])

    def chunk_step(n, state_t):
        r = pl.multiple_of(n * A_CHUNK, A_CHUNK)
        rows = pl.ds(r, A_CHUNK)
        qd = qdec_s[rows, :]
        o = lax.dot_general(qd, state_t.astype(BF16), NT_DIMS, preferred_element_type=F32)
        oacc_s[rows, :] += o
        upd = lax.dot_general(i_ref[rows, :], kend_s[rows, :], TN_DIMS,
                              preferred_element_type=F32)
        return dec_s[pl.ds(r, 1), :] * state_t + upd

    lax.fori_loop(0, seq // A_CHUNK, chunk_step, jnp.zeros((A_DK, A_DK), F32),
                  unroll=HG_UNROLL)

    o = oacc_s[...]
    o = o * lax.rsqrt(jnp.mean(o * o, axis=-1, keepdims=True) + NORM_EPS) * onw_ref[...]
    g = g_ref[...].astype(F32)
    o_ref[...] = (o * (g * jax.nn.sigmoid(g))).astype(o_ref.dtype)


def _hgrn(qa, fa, ia, ga, lb, onw, bsz, seq):
    view = lambda t: t.reshape(4, bsz, seq, 128)
    head_spec = pl.BlockSpec((None, None, seq, 128), lambda b, h: (h, b, 0, 0))
    vec_spec = pl.BlockSpec((None, 1, 128), lambda b, h: (h, 0, 0))
    return pl.pallas_call(
        _hgrn_kernel,
        grid=(bsz, A_HEADS),
        in_specs=[head_spec] * 4 + [vec_spec] * 2,
        out_specs=pl.BlockSpec((None, seq, 128), lambda b, h: (b, 0, h)),
        out_shape=jax.ShapeDtypeStruct((bsz, seq, A_WIDTH), BF16),
        scratch_shapes=[pltpu.VMEM((seq, 128), BF16), pltpu.VMEM((seq, 128), BF16),
                        pltpu.VMEM((seq, 128), F32), pltpu.VMEM((seq, 128), F32)],
        compiler_params=_cparams(2),
        name="hgrn2",
    )(view(qa), view(fa), view(ia), view(ga), lb.reshape(4, 1, 128), onw.reshape(4, 1, 128))


def _dil_kernel(q_ref, k_ref, v_ref, o_ref, q_s, k_s, v_s, acc_s, m_s, l_s):
    q_s[...] = q_ref[...].astype(F32)
    k_s[...] = k_ref[...].astype(F32)
    v_s[...] = v_ref[...].astype(F32)

    iq = lax.broadcasted_iota(jnp.int32, (B_SPAN, 2 * B_SPAN), 0)
    ik = lax.broadcasted_iota(jnp.int32, (B_SPAN, 2 * B_SPAN), 1)
    mask_first = (ik <= iq)[:, :B_SPAN]
    mask_win = (ik >= iq) & (ik <= iq + B_SPAN)

    def attend(q_rows, kv_rows, mask, first):
        q = q_s[q_rows, :].astype(BF16)
        kk = k_s[kv_rows, :].astype(BF16)
        vv = v_s[kv_rows, :].astype(BF16)
        s = lax.dot_general(q, kk, NT_DIMS, preferred_element_type=F32)
        s = jnp.where(mask, s, -jnp.inf)
        m_blk = jnp.max(s, axis=-1, keepdims=True)
        if first:
            m_new = jnp.broadcast_to(m_blk, (B_SPAN, 128))
        else:
            m_old = m_s[q_rows, :]
            m_new = jnp.maximum(m_old, m_blk)
        reps = s.shape[1] // 128
        p = jnp.exp(s - jnp.concatenate([m_new] * reps, axis=1))
        l_blk = jnp.sum(p, axis=-1, keepdims=True)
        pv = _dot(p.astype(BF16), vv)
        if first:
            l_s[q_rows, :] = jnp.broadcast_to(l_blk, (B_SPAN, 128))
            acc_s[q_rows, :] = pv
        else:
            alpha = jnp.exp(m_old - m_new)
            l_s[q_rows, :] = alpha * l_s[q_rows, :] + l_blk
            acc_s[q_rows, :] = alpha * acc_s[q_rows, :] + pv
        m_s[q_rows, :] = m_new

    for dil in B_DILS:
        first = dil == 1
        blocks = q_ref.shape[0] // (dil * B_SPAN)
        stride = None if dil == 1 else dil

        def head_blocks(i, _, dil=dil, first=first, stride=stride):
            for u in range(min(dil, DIL_UNROLL)):
                r = i * min(dil, DIL_UNROLL) + u
                attend(pl.ds(r, B_SPAN, stride=stride), pl.ds(r, B_SPAN, stride=stride),
                       mask_first, first)
            return 0

        group = 5 if dil == 1 else dil

        def window_blocks(i, _, dil=dil, first=first, stride=stride, group=group):
            for u in range(group):
                if dil == 1:
                    q0 = pl.multiple_of((i * group + u + 1) * B_SPAN, B_SPAN)
                else:
                    q0 = u + (i + 1) * (dil * B_SPAN)
                k0 = q0 - dil * B_SPAN
                if dil == 1:
                    k0 = pl.multiple_of(k0, B_SPAN)
                attend(pl.ds(q0, B_SPAN, stride=stride), pl.ds(k0, 2 * B_SPAN, stride=stride),
                       mask_win, first)
            return 0

        lax.fori_loop(0, max(1, dil // DIL_UNROLL), head_blocks, 0)
        if blocks > 1:
            lax.fori_loop(0, dil * (blocks - 1) // group, window_blocks, 0)

    o_ref[...] = (acc_s[...] / l_s[...]).astype(o_ref.dtype)


def _dil_attn(qb, kb, vb, bsz, seq):
    view = lambda t: t.reshape(4, bsz, seq, 128)
    head_spec = pl.BlockSpec((None, None, seq, 128), lambda b, h: (h, b, 0, 0))
    return pl.pallas_call(
        _dil_kernel,
        grid=(bsz, B_HEADS),
        in_specs=[head_spec] * 3,
        out_specs=pl.BlockSpec((None, seq, 128), lambda b, h: (b, 0, h)),
        out_shape=jax.ShapeDtypeStruct((bsz, seq, B_WIDTH), BF16),
        scratch_shapes=[pltpu.VMEM((seq, 128), F32)] * 6,
        compiler_params=_cparams(2),
        name="dilated_attn",
    )(view(qb), view(kb), view(vb))


def _out_proj_kernel(oa_ref, ob_ref, w_ref, nw_ref, x_ref, o_ref):
    y = _dot(oa_ref[...], w_ref[:A_WIDTH, :]) + _dot(ob_ref[...], w_ref[A_WIDTH:, :])
    o_ref[...] = x_ref[...] + _rms(y, nw_ref[...])


def _out_proj(oa, ob, w_out, nw, x2):
    n = x2.shape[0]
    row = lambda w: pl.BlockSpec((TM, w), lambda i: (i, 0))
    return pl.pallas_call(
        _out_proj_kernel,
        grid=(n // TM,),
        in_specs=[row(A_WIDTH), row(B_WIDTH), _resident((D_MODEL, D_MODEL)),
                  _resident((1, D_MODEL)), row(D_MODEL)],
        out_specs=row(D_MODEL),
        out_shape=jax.ShapeDtypeStruct((n, D_MODEL), F32),
        compiler_params=_cparams(1),
        name="out_proj",
    )(oa, ob, w_out, nw, x2)


def _kv_kernel(mem_ref, nw_ref, w_ref, k_ref, v_ref):
    h = _rms(mem_ref[...], nw_ref[...]).astype(BF16)
    k_ref[...] = _dot(h, w_ref[:, :D_MODEL]).astype(BF16)
    v_ref[...] = _dot(h, w_ref[:, D_MODEL:]).astype(BF16)


def _kv_proj(mem, nw, wkv):
    bsz = mem.shape[0]
    blk = pl.BlockSpec((None, MEM_LEN, D_MODEL), lambda b: (b, 0, 0))
    kv = jax.ShapeDtypeStruct((bsz, MEM_LEN, D_MODEL), BF16)
    return pl.pallas_call(
        _kv_kernel,
        grid=(bsz,),
        in_specs=[blk, _resident((1, D_MODEL)), _resident((D_MODEL, 2 * D_MODEL))],
        out_specs=[blk, blk],
        out_shape=[kv, kv],
        compiler_params=_cparams(1),
        name="xattn_kv",
    )(mem, nw, wkv)


def _xattn_kernel(x_ref, k_ref, v_ref, wq_ref, wo_ref, nwa_ref, nwb_ref, o_ref):
    x = x_ref[...]
    h = _rms(x, nwa_ref[...]).astype(BF16)
    q = _dot(h, wq_ref[...])
    heads = []
    for hd in range(X_HEADS):
        cols = slice(hd * X_HD, (hd + 1) * X_HD)
        qh = (q[:, cols] * (X_HD ** -0.5)).astype(BF16)
        s = lax.dot_general(qh, k_ref[:, cols], NT_DIMS, preferred_element_type=F32)
        p = jnp.exp(s - jnp.max(s, axis=-1, keepdims=True))
        l = jnp.sum(p, axis=-1, keepdims=True)
        heads.append((_dot(p.astype(BF16), v_ref[:, cols]) / l).astype(BF16))
    y = _dot(jnp.concatenate(heads, axis=1), wo_ref[...])
    o_ref[...] = x + _rms(y, nwb_ref[...])


def _xattn(x3, k, v, wq, wo, nwa, nwb):
    bsz, seq, _ = x3.shape
    row = pl.BlockSpec((None, TM, D_MODEL), lambda b, i: (b, i, 0))
    kvs = pl.BlockSpec((None, MEM_LEN, D_MODEL), lambda b, i: (b, 0, 0))
    return pl.pallas_call(
        _xattn_kernel,
        grid=(bsz, seq // TM),
        in_specs=[row, kvs, kvs, _resident((D_MODEL, D_MODEL)), _resident((D_MODEL, D_MODEL)),
                  _resident((1, D_MODEL)), _resident((1, D_MODEL))],
        out_specs=row,
        out_shape=jax.ShapeDtypeStruct(x3.shape, F32),
        compiler_params=_cparams(2),
        name="xattn",
    )(x3, k, v, wq, wo, nwa, nwb)


def _ffn_kernel(x_ref, nwa_ref, w_in_ref, w_out_ref, nwb_ref, o_ref):
    x = x_ref[...]
    h = _rms(x, nwa_ref[...]).astype(BF16)
    y = None
    for j in range(D_FF // FF_CHUNK):
        z1 = _dot(h, w_in_ref[:, j * FF_CHUNK:(j + 1) * FF_CHUNK])
        z2 = _dot(h, w_in_ref[:, D_FF + j * FF_CHUNK:D_FF + (j + 1) * FF_CHUNK])
        a = (z1 * jax.nn.sigmoid(z1) * z2).astype(BF16)
        part = _dot(a, w_out_ref[j * FF_CHUNK:(j + 1) * FF_CHUNK, :])
        y = part if y is None else y + part
    o_ref[...] = x + _rms(y, nwb_ref[...])


def _ffn(x2, nwa, w_in, w_out, nwb):
    n = x2.shape[0]
    row = pl.BlockSpec((TM, D_MODEL), lambda i: (i, 0))
    return pl.pallas_call(
        _ffn_kernel,
        grid=(n // TM,),
        in_specs=[row, _resident((1, D_MODEL)), _resident((D_MODEL, 2 * D_FF)),
                  _resident((D_FF, D_MODEL)), _resident((1, D_MODEL))],
        out_specs=row,
        out_shape=jax.ShapeDtypeStruct((n, D_MODEL), F32),
        compiler_params=_cparams(1),
        name="ffn",
    )(x2, nwa, w_in, w_out, nwb)


def _norm_cast_kernel(x_ref, nw_ref, o_ref):
    o_ref[...] = _rms(x_ref[...], nw_ref[...]).astype(o_ref.dtype)


def _norm_cast(x2, nw):
    n = x2.shape[0]
    row = pl.BlockSpec((TM, D_MODEL), lambda i: (i, 0))
    return pl.pallas_call(
        _norm_cast_kernel,
        grid=(n // TM,),
        in_specs=[row, _resident((1, D_MODEL))],
        out_specs=row,
        out_shape=jax.ShapeDtypeStruct((n, D_MODEL), BF16),
        compiler_params=_cparams(1),
        name="norm_cast",
    )(x2, nw)


def _s5_kernel(u_ref, w1_ref, p_ref, a_ref, y_ref, dh_s, hprev_s, *, bsz):
    feat = S5_T * C_GROUP
    r1 = _dot(u_ref[...], w1_ref[...])
    y_ref[...] = r1[:, :feat]
    dh_s[...] = r1[:, feat:]
    a1 = a_ref[0:1, :]
    a2 = a_ref[1:2, :]
    a2s = a_ref[2:3, :]
    n_chunks = u_ref.shape[0] // bsz

    def step(n, carry):
        h, hs = carry
        rows = pl.ds(pl.multiple_of(n * bsz, bsz), bsz)
        hprev_s[rows, :] = h
        d = dh_s[rows, :]
        return (a1 * h + a2 * hs + d[:, :2 * C_STATE],
                a1 * hs + a2s * h + d[:, 2 * C_STATE:])

    zero = jnp.zeros((bsz, 2 * C_STATE), F32)
    lax.fori_loop(0, n_chunks, step, (zero, zero))
    y_ref[...] += _dot(hprev_s[...].astype(BF16), p_ref[...])


def _s5_scan(u_g, w1, pmat, avec, bsz):
    groups, rows, feat = u_g.shape
    return pl.pallas_call(
        functools.partial(_s5_kernel, bsz=bsz),
        grid=(groups,),
        in_specs=[pl.BlockSpec((None, rows, feat), lambda g: (g, 0, 0)),
                  pl.BlockSpec((None, feat, feat + 4 * C_STATE), lambda g: (g, 0, 0)),
                  pl.BlockSpec((None, 2 * C_STATE, feat), lambda g: (g, 0, 0)),
                  pl.BlockSpec((None, 8, 2 * C_STATE), lambda g: (g, 0, 0))],
        out_specs=pl.BlockSpec((None, rows, feat), lambda g: (g, 0, 0)),
        out_shape=jax.ShapeDtypeStruct((groups, rows, feat), F32),
        scratch_shapes=[pltpu.VMEM((rows, 4 * C_STATE), F32),
                        pltpu.VMEM((rows, 2 * C_STATE), F32)],
        compiler_params=_cparams(1),
        name="s5_scan",
    )(u_g, w1, pmat, avec)


def _s5_weights(lam_re, lam_im, log_dt, b_re, b_im, c_re, c_im):
    hp = lax.Precision.HIGHEST
    lr = jnp.minimum(lam_re.astype(F32), C_MIN_NEG_RE)
    li = lam_im.astype(F32)
    dt = jnp.exp(log_dt.astype(F32))[:, None]
    mag = jnp.exp(dt * lr)
    ar, ai = mag * jnp.cos(dt * li), mag * jnp.sin(dt * li)
    den = lr * lr + li * li
    zr = ((ar - 1.0) * lr + ai * li) / den
    zi = (ai * lr - (ar - 1.0) * li) / den
    br, bi = b_re.astype(F32), b_im.astype(F32)
    bbr = zr[..., None] * br - zi[..., None] * bi
    bbi = zr[..., None] * bi + zi[..., None] * br
    cr, ci = c_re.astype(F32), c_im.astype(F32)
    kk = jnp.arange(S5_T + 1, dtype=F32)[:, None, None]
    pmag = jnp.exp(kk * (dt * lr))
    pr, pi_ = pmag * jnp.cos(kk * (dt * li)), pmag * jnp.sin(kk * (dt * li))
    car = cr[None] * pr[:, :, None, :] - ci[None] * pi_[:, :, None, :]
    cai = cr[None] * pi_[:, :, None, :] + ci[None] * pr[:, :, None, :]
    ker = (jnp.einsum('kgcp,gpd->kgcd', car, bbr, precision=hp)
           - jnp.einsum('kgcp,gpd->kgcd', cai, bbi, precision=hp))
    s_idx = jnp.arange(S5_T)[:, None]
    t_idx = jnp.arange(S5_T)[None, :]
    lag = t_idx - s_idx
    toe = jnp.where((lag >= 0)[:, :, None, None, None], ker[jnp.clip(lag, 0)], 0.0)
    feat = S5_T * C_GROUP
    m_mat = toe.transpose(2, 0, 4, 1, 3).reshape(C_GROUPS, feat, feat)
    rev = S5_T - 1 - jnp.arange(S5_T)
    qr = pr[rev][..., None] * bbr[None] - pi_[rev][..., None] * bbi[None]
    qi = pr[rev][..., None] * bbi[None] + pi_[rev][..., None] * bbr[None]
    to_rows = lambda t: t.transpose(1, 0, 3, 2).reshape(C_GROUPS, feat, C_STATE)
    qr, qi = to_rows(qr), to_rows(qi)
    w1 = jnp.concatenate([m_mat, qr, qi, qi, qr], axis=-1)
    to_cols = lambda t: t.transpose(1, 3, 0, 2).reshape(C_GROUPS, C_STATE, feat)
    pmat = jnp.concatenate([to_cols(car[1:]), -to_cols(cai[1:])], axis=1)
    atr, ati = pr[S5_T], pi_[S5_T]
    zeros = jnp.zeros_like(atr)
    avec = jnp.stack([jnp.concatenate([atr, atr], -1), jnp.concatenate([-ati, ati], -1),
                      jnp.concatenate([ati, -ati], -1)] + [jnp.concatenate([zeros, zeros], -1)] * 5,
                     axis=1)
    return w1.astype(BF16), pmat.astype(BF16), avec


def _glu_kernel(y_ref, x_ref, nwa_ref, d_ref, w_ref, nwb_ref, o_ref):
    x = x_ref[...]
    u = _rms(x, nwa_ref[...])
    t = y_ref[...] + d_ref[...] * u
    gl = (0.5 * t * (1.0 + lax.erf(t * math.sqrt(0.5)))).astype(BF16)
    z = _dot(gl, w_ref[...])
    y = z[:, :D_MODEL] * jax.nn.sigmoid(z[:, D_MODEL:])
    o_ref[...] = x + _rms(y, nwb_ref[...])


def _glu(y2, x2, nwa, d, w_glu, nwb):
    n = x2.shape[0]
    row = pl.BlockSpec((TM, D_MODEL), lambda i: (i, 0))
    return pl.pallas_call(
        _glu_kernel,
        grid=(n // TM,),
        in_specs=[row, row, _resident((1, D_MODEL)), _resident((1, D_MODEL)),
                  _resident((D_MODEL, 2 * D_MODEL)), _resident((1, D_MODEL))],
        out_specs=row,
        out_shape=jax.ShapeDtypeStruct((n, D_MODEL), F32),
        compiler_params=_cparams(1),
        name="s5_glu",
    )(y2, x2, nwa, d, w_glu, nwb)


def _rotary_tables(seq):
    half = B_HD // 2
    inv_freq = ROPE_THETA ** (-jnp.arange(half, dtype=F32) / half)
    ang = jnp.arange(seq, dtype=jnp.int32).astype(F32)[:, None] * inv_freq[None, :]
    cos, sin = jnp.cos(ang), jnp.sin(ang)
    return jnp.concatenate([cos, cos], axis=-1), jnp.concatenate([-sin, sin], axis=-1)


def kernel(x, mem, norm_w, mem_norm_w, ab_w_in, ab_w_out, hgrn_lb_logits, hgrn_out_norm_w, s5_lambda_re, s5_lambda_im, s5_log_dt, s5_b_re, s5_b_im, s5_c_re, s5_c_im, s5_d, s5_w_glu, xattn_wq, xattn_wkv, xattn_wo, ffn_w_in, ffn_w_out):
    bsz, seq, _ = x.shape
    n = bsz * seq
    depth = norm_w.shape[0]
    nw = lambda layer, i: norm_w[layer, i].astype(F32).reshape(1, D_MODEL)
    lb_table = jnp.cumsum(jax.nn.softmax(hgrn_lb_logits.astype(F32), axis=0), axis=0)
    cos_t, sin_t = _rotary_tables(seq)
    x2 = x.reshape(n, D_MODEL)
    for layer in range(depth):
        j = layer // 2
        if layer % 2 == 0:
            qa, fa, ia, ga, qb, kb, vb = _proj_ab(
                x2, nw(layer, 0), ab_w_in[j].astype(BF16), cos_t, sin_t)
            oa = _hgrn(qa, fa, ia, ga, lb_table[layer], hgrn_out_norm_w[j].astype(F32), bsz, seq)
            ob = _dil_attn(qb, kb, vb, bsz, seq)
            x2 = _out_proj(oa.reshape(n, A_WIDTH), ob.reshape(n, B_WIDTH),
                           ab_w_out[j].astype(BF16), nw(layer, 1), x2)
        else:
            w1, pmat, avec = _s5_weights(s5_lambda_re[j], s5_lambda_im[j], s5_log_dt[j],
                                         s5_b_re[j], s5_b_im[j], s5_c_re[j], s5_c_im[j])
            hn = _norm_cast(x2, nw(layer, 0))
            n_chunks = seq // S5_T
            u_g = (hn.reshape(bsz, n_chunks, S5_T, C_GROUPS, C_GROUP)
                   .transpose(3, 1, 0, 2, 4).reshape(C_GROUPS, n_chunks * bsz, S5_T * C_GROUP))
            y_g = _s5_scan(u_g, w1, pmat, avec, bsz)
            y2 = (y_g.reshape(C_GROUPS, n_chunks, bsz, S5_T, C_GROUP)
                  .transpose(2, 1, 3, 0, 4).reshape(n, D_MODEL))
            x2 = _glu(y2, x2, nw(layer, 0), s5_d[j].astype(F32).reshape(1, D_MODEL),
                      s5_w_glu[j].astype(BF16), nw(layer, 1))
        k, v = _kv_proj(mem, mem_norm_w[layer].astype(F32).reshape(1, D_MODEL),
                        xattn_wkv[layer].astype(BF16))
        x2 = _xattn(x2.reshape(bsz, seq, D_MODEL), k, v, xattn_wq[layer].astype(BF16),
                    xattn_wo[layer].astype(BF16), nw(layer, 2), nw(layer, 3)).reshape(n, D_MODEL)
        x2 = _ffn(x2, nw(layer, 4), ffn_w_in[layer].astype(BF16),
                  ffn_w_out[layer].astype(BF16), nw(layer, 5))
    return x2.reshape(bsz, seq, D_MODEL)
```

```python
import math

import jax
import jax.numpy as jnp
from jax import lax
from jax.experimental import pallas as pl
from jax.experimental.pallas import tpu as pltpu

F32 = jnp.float32
BF16 = jnp.bfloat16

D_MODEL = 1024
NORM_EPS = 1e-6
A_HEADS = 4
A_DK = 128
A_WIDTH = 512
A_CHUNK = 32
B_HEADS = 4
B_HD = 128
B_WIDTH = 512
B_SPAN = 128
B_DILS = (1, 4, 16)
DIL_UNROLL = 4
ROPE_THETA = 10000.0
IN_AB_WIDTH = 4 * A_WIDTH + 3 * B_WIDTH
C_GROUP = 16
C_GROUPS = 64
C_STATE = 64
C_MIN_NEG_RE = -1e-4
MEM_LEN = 256
X_HEADS = 4
X_HD = 256
D_FF = 2816

TM = 512
FF_CHUNK = 1408
VMEM_LIMIT = 56 * 1024 * 1024

NT_DIMS = (((1,), (1,)), ((), ()))
TN_DIMS = (((0,), (0,)), ((), ()))


def _cparams(n_axes):
    return pltpu.CompilerParams(
        dimension_semantics=("arbitrary",) * n_axes, vmem_limit_bytes=VMEM_LIMIT)


def _resident(shape):
    zeros = (0,) * len(shape)
    return pl.BlockSpec(shape, lambda *_: zeros, pipeline_mode=pl.Buffered(1))


def _rms(xf, w):
    return xf * lax.rsqrt(jnp.mean(xf * xf, axis=-1, keepdims=True) + NORM_EPS) * w


def _dot(a, b):
    return jnp.dot(a, b, preferred_element_type=F32)


def _proj_ab_kernel(x_ref, nw_ref, w_ref, cos_ref, sin_ref,
                    qa_ref, fa_ref, ia_ref, ga_ref, qb_ref, kb_ref, vb_ref):
    h = _rms(x_ref[...], nw_ref[...]).astype(BF16)
    cos = cos_ref[...]
    sin = sin_ref[...]
    outs = (qa_ref, fa_ref, ia_ref, ga_ref, qb_ref, kb_ref, vb_ref)
    for j, o_ref in enumerate(outs):
        z = _dot(h, w_ref[:, j * 512:(j + 1) * 512])
        for hd in range(4):
            zh = z[:, hd * 128:(hd + 1) * 128]
            if j in (4, 5):
                zh = zh * cos + pltpu.roll(zh, 64, 1) * sin
            if j == 4:
                zh = zh * (B_HD ** -0.5)
            o_ref[hd] = zh.astype(o_ref.dtype)


def _proj_ab(x2, nw, w_in, cos_t, sin_t):
    n = x2.shape[0]
    pos_blocks = cos_t.shape[0] // TM
    head_out = lambda dt: jax.ShapeDtypeStruct((4, n, 128), dt)
    head_spec = pl.BlockSpec((4, TM, 128), lambda i: (0, i, 0))
    return pl.pallas_call(
        _proj_ab_kernel,
        grid=(n // TM,),
        in_specs=[
            pl.BlockSpec((TM, D_MODEL), lambda i: (i, 0)),
            _resident((1, D_MODEL)),
            _resident((D_MODEL, IN_AB_WIDTH)),
            pl.BlockSpec((TM, 128), lambda i: (i % pos_blocks, 0)),
            pl.BlockSpec((TM, 128), lambda i: (i % pos_blocks, 0)),
        ],
        out_specs=[head_spec] * 7,
        out_shape=[head_out(BF16), head_out(F32), head_out(BF16), head_out(BF16),
                   head_out(BF16), head_out(BF16), head_out(BF16)],
        compiler_params=_cparams(1),
        name="proj_ab",
    )(x2, nw, w_in, cos_t, sin_t)


HG_BLK = 256
HG_UNROLL = 4


def _hgrn_kernel(q_ref, f_ref, i_ref, g_ref, lb_ref, onw_ref, o_ref,
                 qdec_s, kend_s, dec_s, oacc_s, upd_s, state_s):
    seq = q_ref.shape[0]
    lb = lb_ref[...]
    ri = lax.broadcasted_iota(jnp.int32, (2 * HG_BLK, HG_BLK), 0)
    ci = lax.broadcasted_iota(jnp.int32, (2 * HG_BLK, HG_BLK), 1)
    rr = jnp.where(ri >= HG_BLK, ri - HG_BLK, ri)
    same = (rr // A_CHUNK) == (ci // A_CHUNK)
    lmat = jnp.where(same & ((ci <= rr) | (ri >= HG_BLK)), 1.0, 0.0).astype(BF16)
    causal = (same & (ci <= rr))[:HG_BLK]

    for blk in range(seq // HG_BLK):
        rows = pl.ds(blk * HG_BLK, HG_BLK)
        f = lb + (1.0 - lb) * jax.nn.sigmoid(f_ref[rows, :])
        log_f = jnp.log(f)
        k = 1.0 - f
        hi = log_f.astype(BF16)
        lo = (log_f - hi.astype(F32)).astype(BF16)
        bb = _dot(lmat, hi) + _dot(lmat, lo)
        b = bb[:HG_BLK]
        b_last = bb[HG_BLK:]
        q_dec = (q_ref[rows, :].astype(F32) * jnp.exp(b)).astype(BF16)
        k_inv = (k * jnp.exp(-b)).astype(BF16)
        qdec_s[rows, :] = q_dec
        kend_s[rows, :] = (k * jnp.exp(b_last - b)).astype(BF16)
        dec_s[rows, :] = jnp.exp(b_last)
        scores = lax.dot_general(q_dec, k_inv, NT_DIMS, preferred_element_type=F32)
        scores = jnp.where(causal, scores, 0.0).astype(BF16)
        oacc_s[rows, :] = _dot(scores, i_ref[rows, :])

    def increment(n, _):
        rows = pl.ds(pl.multiple_of(n * A_CHUNK, A_CHUNK), A_CHUNK)
        upd_s[n] = lax.dot_general(i_ref[rows, :], kend_s[rows, :], TN_DIMS,
                                   preferred_element_type=F32)
        return 0

    def advance(n, state_t):
        state_s[n] = state_t.astype(BF16)
        dec = dec_s[pl.ds(pl.multiple_of(n * A_CHUNK, A_CHUNK), 1), :]
        return dec * state_t + upd_s[n]

    def readout(n, _):
        rows = pl.ds(pl.multiple_of(n * A_CHUNK, A_CHUNK), A_CHUNK)
        oacc_s[rows, :] += lax.dot_general(qdec_s[rows, :], state_s[n], NT_DIMS,
                                           preferred_element_type=F32)
        return 0

    n_chunks = seq // A_CHUNK
    lax.fori_loop(0, n_chunks, increment, 0, unroll=HG_UNROLL)
    lax.fori_loop(0, n_chunks, advance, jnp.zeros((A_DK, A_DK), F32), unroll=HG_UNROLL)
    lax.fori_loop(0, n_chunks, readout, 0, unroll=HG_UNROLL)

    o = oacc_s[...]
    o = o * lax.rsqrt(jnp.mean(o * o, axis=-1, keepdims=True) + NORM_EPS) * onw_ref[...]
    g = g_ref[...].astype(F32)
    o_ref[...] = (o * (g * jax.nn.sigmoid(g))).astype(o_ref.dtype)


def _hgrn(qa, fa, ia, ga, lb, onw, bsz, seq):
    view = lambda t: t.reshape(4, bsz, seq, 128)
    head_spec = pl.BlockSpec((None, None, seq, 128), lambda b, h: (h, b, 0, 0))
    vec_spec = pl.BlockSpec((None, 1, 128), lambda b, h: (h, 0, 0))
    return pl.pallas_call(
        _hgrn_kernel,
        grid=(bsz, A_HEADS),
        in_specs=[head_spec] * 4 + [vec_spec] * 2,
        out_specs=pl.BlockSpec((None, seq, 128), lambda b, h: (b, 0, h)),
        out_shape=jax.ShapeDtypeStruct((bsz, seq, A_WIDTH), BF16),
        scratch_shapes=[pltpu.VMEM((seq, 128), BF16), pltpu.VMEM((seq, 128), BF16),
                        pltpu.VMEM((seq, 128), F32), pltpu.VMEM((seq, 128), F32),
                        pltpu.VMEM((seq // A_CHUNK, A_DK, A_DK), F32),
                        pltpu.VMEM((seq // A_CHUNK, A_DK, A_DK), BF16)],
        compiler_params=_cparams(2),
        name="hgrn2",
    )(view(qa), view(fa), view(ia), view(ga), lb.reshape(4, 1, 128), onw.reshape(4, 1, 128))


def _dil_kernel(q_ref, k_ref, v_ref, o_ref, q_s, k_s, v_s, qp_s, kp_s, vp_s, acc_s, m_s, l_s):
    seq = q_ref.shape[0]
    q_s[...] = q_ref[...].astype(F32)
    k_s[...] = k_ref[...].astype(F32)
    v_s[...] = v_ref[...].astype(F32)

    iq = lax.broadcasted_iota(jnp.int32, (B_SPAN, 2 * B_SPAN), 0)
    ik = lax.broadcasted_iota(jnp.int32, (B_SPAN, 2 * B_SPAN), 1)
    mask_first = (ik <= iq)[:, :B_SPAN]
    mask_win = (ik >= iq) & (ik <= iq + B_SPAN)

    def attend(p0, n_keys, q_rows, mask, first):
        q = qp_s[pl.ds(p0, B_SPAN), :]
        k0 = p0 + B_SPAN - n_keys
        if not isinstance(k0, int):
            k0 = pl.multiple_of(k0, B_SPAN)
        kk = kp_s[pl.ds(k0, n_keys), :]
        vv = vp_s[pl.ds(k0, n_keys), :]
        s = lax.dot_general(q, kk, NT_DIMS, preferred_element_type=F32)
        s = jnp.where(mask, s, -jnp.inf)
        m_blk = jnp.max(s, axis=-1, keepdims=True)
        if first:
            m_new = jnp.broadcast_to(m_blk, (B_SPAN, 128))
        else:
            m_old = m_s[q_rows, :]
            m_new = jnp.maximum(m_old, m_blk)
        reps = s.shape[1] // 128
        p = jnp.exp(s - jnp.concatenate([m_new] * reps, axis=1))
        l_blk = jnp.sum(p, axis=-1, keepdims=True)
        pv = _dot(p.astype(BF16), vv)
        if first:
            l_s[q_rows, :] = jnp.broadcast_to(l_blk, (B_SPAN, 128))
            acc_s[q_rows, :] = pv
        else:
            alpha = jnp.exp(m_old - m_new)
            l_s[q_rows, :] = alpha * l_s[q_rows, :] + l_blk
            acc_s[q_rows, :] = alpha * acc_s[q_rows, :] + pv
        m_s[q_rows, :] = m_new

    for dil in B_DILS:
        first = dil == 1
        cls = seq // dil
        blocks = cls // B_SPAN
        stride = None if dil == 1 else dil

        def permute(r, _, dil=dil, cls=cls, stride=stride):
            src = pl.ds(r, cls, stride=stride)
            dst = pl.ds(pl.multiple_of(r * cls, B_SPAN), cls)
            qp_s[dst, :] = q_s[src, :].astype(BF16)
            kp_s[dst, :] = k_s[src, :].astype(BF16)
            vp_s[dst, :] = v_s[src, :].astype(BF16)
            return 0

        if dil == 1:
            qp_s[...] = q_ref[...]
            kp_s[...] = k_ref[...]
            vp_s[...] = v_ref[...]
        else:
            lax.fori_loop(0, dil, permute, 0)

        def head_blocks(i, _, dil=dil, cls=cls, first=first, stride=stride):
            for u in range(min(dil, DIL_UNROLL)):
                if dil == 1:
                    attend(0, B_SPAN, pl.ds(0, B_SPAN), mask_first, first)
                else:
                    r = i * DIL_UNROLL + u
                    attend(pl.multiple_of(r * cls, B_SPAN), B_SPAN,
                           pl.ds(r, B_SPAN, stride=stride), mask_first, first)
            return 0

        group = 5 if dil == 1 else dil

        def window_blocks(i, _, dil=dil, cls=cls, first=first, stride=stride, group=group):
            for u in range(group):
                if dil == 1:
                    r, blk = 0, i * group + u + 1
                    q_rows = pl.ds(pl.multiple_of(blk * B_SPAN, B_SPAN), B_SPAN)
                else:
                    r, blk = u, i + 1
                    q_rows = pl.ds(r + blk * (dil * B_SPAN), B_SPAN, stride=stride)
                attend(pl.multiple_of(r * cls + blk * B_SPAN, B_SPAN), 2 * B_SPAN,
                       q_rows, mask_win, first)
            return 0

        lax.fori_loop(0, max(1, dil // DIL_UNROLL), head_blocks, 0)
        if blocks > 1:
            lax.fori_loop(0, dil * (blocks - 1) // group, window_blocks, 0)

    o_ref[...] = (acc_s[...] / l_s[...]).astype(o_ref.dtype)


def _dil_attn(qb, kb, vb, bsz, seq):
    view = lambda t: t.reshape(4, bsz, seq, 128)
    head_spec = pl.BlockSpec((None, None, seq, 128), lambda b, h: (h, b, 0, 0))
    return pl.pallas_call(
        _dil_kernel,
        grid=(bsz, B_HEADS),
        in_specs=[head_spec] * 3,
        out_specs=pl.BlockSpec((None, seq, 128), lambda b, h: (b, 0, h)),
        out_shape=jax.ShapeDtypeStruct((bsz, seq, B_WIDTH), BF16),
        scratch_shapes=([pltpu.VMEM((seq, 128), F32)] * 3 + [pltpu.VMEM((seq, 128), BF16)] * 3
                        + [pltpu.VMEM((seq, 128), F32)] * 3),
        compiler_params=_cparams(2),
        name="dilated_attn",
    )(view(qb), view(kb), view(vb))


def _out_proj_kernel(oa_ref, ob_ref, w_ref, nw_ref, x_ref, o_ref):
    y = _dot(oa_ref[...], w_ref[:A_WIDTH, :]) + _dot(ob_ref[...], w_ref[A_WIDTH:, :])
    o_ref[...] = x_ref[...] + _rms(y, nw_ref[...])


def _out_proj(oa, ob, w_out, nw, x2):
    n = x2.shape[0]
    row = lambda w: pl.BlockSpec((TM, w), lambda i: (i, 0))
    return pl.pallas_call(
        _out_proj_kernel,
        grid=(n // TM,),
        in_specs=[row(A_WIDTH), row(B_WIDTH), _resident((D_MODEL, D_MODEL)),
                  _resident((1, D_MODEL)), row(D_MODEL)],
        out_specs=row(D_MODEL),
        out_shape=jax.ShapeDtypeStruct((n, D_MODEL), F32),
        compiler_params=_cparams(1),
        name="out_proj",
    )(oa, ob, w_out, nw, x2)


def _kv_kernel(mem_ref, nw_ref, w_ref, k_ref, v_ref):
    h = _rms(mem_ref[...], nw_ref[...]).astype(BF16)
    k_ref[...] = _dot(h, w_ref[:, :D_MODEL]).astype(BF16)
    v_ref[...] = _dot(h, w_ref[:, D_MODEL:]).astype(BF16)


def _kv_proj(mem, nw, wkv):
    bsz = mem.shape[0]
    blk = pl.BlockSpec((None, MEM_LEN, D_MODEL), lambda b: (b, 0, 0))
    kv = jax.ShapeDtypeStruct((bsz, MEM_LEN, D_MODEL), BF16)
    return pl.pallas_call(
        _kv_kernel,
        grid=(bsz,),
        in_specs=[blk, _resident((1, D_MODEL)), _resident((D_MODEL, 2 * D_MODEL))],
        out_specs=[blk, blk],
        out_shape=[kv, kv],
        compiler_params=_cparams(1),
        name="xattn_kv",
    )(mem, nw, wkv)


def _xattn_kernel(x_ref, k_ref, v_ref, wq_ref, wo_ref, nwa_ref, nwb_ref, o_ref):
    x = x_ref[...]
    h = _rms(x, nwa_ref[...]).astype(BF16)
    q = _dot(h, wq_ref[...])
    heads = []
    for hd in range(X_HEADS):
        cols = slice(hd * X_HD, (hd + 1) * X_HD)
        qh = (q[:, cols] * (X_HD ** -0.5)).astype(BF16)
        s = lax.dot_general(qh, k_ref[:, cols], NT_DIMS, preferred_element_type=F32)
        p = jnp.exp(s - jnp.max(s, axis=-1, keepdims=True))
        l = jnp.sum(p, axis=-1, keepdims=True)
        heads.append((_dot(p.astype(BF16), v_ref[:, cols]) / l).astype(BF16))
    y = _dot(jnp.concatenate(heads, axis=1), wo_ref[...])
    o_ref[...] = x + _rms(y, nwb_ref[...])


def _row_spec(time_major, seq):
    if time_major:
        return pl.BlockSpec((TM, D_MODEL), lambda b, i: (i, b))
    tiles = seq // TM
    return pl.BlockSpec((TM, D_MODEL), lambda b, i: (b * tiles + i, 0))


def _stream_shape(time_major, bsz, seq):
    shape = (seq, bsz * D_MODEL) if time_major else (bsz * seq, D_MODEL)
    return jax.ShapeDtypeStruct(shape, F32)


def _xattn(xs, k, v, wq, wo, nwa, nwb, bsz, seq, time_major):
    row = _row_spec(time_major, seq)
    kvs = pl.BlockSpec((None, MEM_LEN, D_MODEL), lambda b, i: (b, 0, 0))
    return pl.pallas_call(
        _xattn_kernel,
        grid=(bsz, seq // TM),
        in_specs=[row, kvs, kvs, _resident((D_MODEL, D_MODEL)), _resident((D_MODEL, D_MODEL)),
                  _resident((1, D_MODEL)), _resident((1, D_MODEL))],
        out_specs=row,
        out_shape=_stream_shape(time_major, bsz, seq),
        compiler_params=_cparams(2),
        name="xattn",
    )(xs, k, v, wq, wo, nwa, nwb)


def _ffn_kernel(x_ref, nwa_ref, w_in_ref, w_out_ref, nwb_ref, o_ref):
    x = x_ref[...]
    h = _rms(x, nwa_ref[...]).astype(BF16)
    y = None
    for j in range(D_FF // FF_CHUNK):
        z1 = _dot(h, w_in_ref[:, j * FF_CHUNK:(j + 1) * FF_CHUNK])
        z2 = _dot(h, w_in_ref[:, D_FF + j * FF_CHUNK:D_FF + (j + 1) * FF_CHUNK])
        a = (z1 * jax.nn.sigmoid(z1) * z2).astype(BF16)
        part = _dot(a, w_out_ref[j * FF_CHUNK:(j + 1) * FF_CHUNK, :])
        y = part if y is None else y + part
    o_ref[...] = x + _rms(y, nwb_ref[...])


def _ffn(xs, nwa, w_in, w_out, nwb, bsz, seq, in_time_major, out_time_major):
    return pl.pallas_call(
        _ffn_kernel,
        grid=(bsz, seq // TM),
        in_specs=[_row_spec(in_time_major, seq), _resident((1, D_MODEL)),
                  _resident((D_MODEL, 2 * D_FF)), _resident((D_FF, D_MODEL)),
                  _resident((1, D_MODEL))],
        out_specs=_row_spec(out_time_major, seq),
        out_shape=_stream_shape(out_time_major, bsz, seq),
        compiler_params=_cparams(2),
        name="ffn",
    )(xs, nwa, w_in, w_out, nwb)


S5_NB = 8
S5_TT = 64
S5_CB = 4
S5_BLK_STATE = 16 * 2 * C_STATE
S5_PAIRS = 8
S5_SCAN_PAIRS = 4


def _s5_kernel(x_ref, nwa_ref, bt_ref, ct_ref, a_ref, d_ref, wg_ref, nwb_ref, o_ref,
               xs_s, v_s, h_s, y_s, st_s, os_s):
    tt = x_ref.shape[0]
    rows = tt * S5_NB
    lane_slabs = D_MODEL // 128

    @pl.when(pl.program_id(1) == 0)
    def _():
        st_s[...] = jnp.zeros_like(st_s)

    for b in range(S5_NB):
        for k in range(lane_slabs):
            lanes = slice(b * D_MODEL + k * 128, b * D_MODEL + (k + 1) * 128)
            xs_s[pl.ds(k * rows + b, tt, stride=S5_NB), :] = x_ref[:, lanes]

    def load_x():
        return jnp.concatenate([xs_s[k * rows:(k + 1) * rows, :] for k in range(lane_slabs)],
                               axis=1)

    ub = _rms(load_x(), nwa_ref[...]).astype(BF16)

    for cb in range(S5_CB):
        v_s[...] = _dot(ub[:, cb * 256:(cb + 1) * 256], bt_ref[cb])
        for m0 in range(0, S5_PAIRS, S5_SCAN_PAIRS):
            pairs = range(m0, m0 + S5_SCAN_PAIRS)
            ar = [jnp.broadcast_to(a_ref[cb, pl.ds(m, 1), :], (S5_NB, 128)) for m in pairs]
            ai = [jnp.broadcast_to(a_ref[cb, pl.ds(S5_PAIRS + m, 1), :], (S5_NB, 128))
                  for m in pairs]
            col = lambda m, im: slice(256 * m + 128 * im, 256 * m + 128 * (im + 1))
            st_col = lambda m, im: slice(cb * S5_BLK_STATE + 256 * m + 128 * im,
                                         cb * S5_BLK_STATE + 256 * m + 128 * (im + 1))
            init = tuple(st_s[:, st_col(m, im)] for m in pairs for im in (0, 1))

            def two_steps(i, carry, pairs=pairs, ar=ar, ai=ai, col=col):
                r0 = pl.multiple_of(i * 2 * S5_NB, 2 * S5_NB)
                out = []
                for idx, m in enumerate(pairs):
                    hr, hi = carry[2 * idx], carry[2 * idx + 1]
                    hist_r, hist_i = [], []
                    for step in range(2):
                        rows = pl.ds(r0 + step * S5_NB, S5_NB)
                        hr, hi = (ar[idx] * hr - ai[idx] * hi + v_s[rows, col(m, 0)],
                                  ar[idx] * hi + ai[idx] * hr + v_s[rows, col(m, 1)])
                        hist_r.append(hr)
                        hist_i.append(hi)
                    both = pl.ds(r0, 2 * S5_NB)
                    h_s[both, col(m, 0)] = jnp.concatenate(hist_r, axis=0).astype(BF16)
                    h_s[both, col(m, 1)] = jnp.concatenate(hist_i, axis=0).astype(BF16)
                    out += [hr, hi]
                return tuple(out)

            final = lax.fori_loop(0, tt // 2, two_steps, init)
            for idx, m in enumerate(pairs):
                st_s[:, st_col(m, 0)] = final[2 * idx]
                st_s[:, st_col(m, 1)] = final[2 * idx + 1]
        y_s[:, cb * 256:(cb + 1) * 256] = _dot(h_s[...], ct_ref[cb])

    x = load_x()
    t = y_s[...] + d_ref[...] * _rms(x, nwa_ref[...])
    gl = (0.5 * t * (1.0 + lax.erf(t * math.sqrt(0.5)))).astype(BF16)
    z = _dot(gl, wg_ref[...])
    y = z[:, :D_MODEL] * jax.nn.sigmoid(z[:, D_MODEL:])
    out = x + _rms(y, nwb_ref[...])
    for k in range(lane_slabs):
        os_s[k * rows:(k + 1) * rows, :] = out[:, k * 128:(k + 1) * 128]
    for b in range(S5_NB):
        for k in range(lane_slabs):
            lanes = slice(b * D_MODEL + k * 128, b * D_MODEL + (k + 1) * 128)
            o_ref[:, lanes] = os_s[pl.ds(k * rows + b, tt, stride=S5_NB), :]


def _s5_block(xs, nwa, bt, ct, avec, d, w_glu, nwb, bsz, seq, tt=S5_TT):
    rows = tt * S5_NB
    tile = pl.BlockSpec((tt, S5_NB * D_MODEL), lambda g, i: (i, g))
    return pl.pallas_call(
        _s5_kernel,
        grid=(bsz // S5_NB, seq // tt),
        in_specs=[tile, _resident((1, D_MODEL)), _resident(bt.shape), _resident(ct.shape),
                  _resident(avec.shape), _resident((1, D_MODEL)),
                  _resident((D_MODEL, 2 * D_MODEL)), _resident((1, D_MODEL))],
        out_specs=tile,
        out_shape=_stream_shape(True, bsz, seq),
        scratch_shapes=[pltpu.VMEM((D_MODEL // 128 * rows, 128), F32),
                        pltpu.VMEM((rows, S5_BLK_STATE), F32),
                        pltpu.VMEM((rows, S5_BLK_STATE), BF16),
                        pltpu.VMEM((rows, D_MODEL), F32),
                        pltpu.VMEM((S5_NB, S5_CB * S5_BLK_STATE), F32),
                        pltpu.VMEM((D_MODEL // 128 * rows, 128), F32)],
        compiler_params=_cparams(2),
        name="s5_block",
    )(xs, nwa, bt, ct, avec, d, w_glu, nwb)


def _s5_weights(lam_re, lam_im, log_dt, b_re, b_im, c_re, c_im):
    lr = jnp.minimum(lam_re.astype(F32), C_MIN_NEG_RE)
    li = lam_im.astype(F32)
    dt = jnp.exp(log_dt.astype(F32))[:, None]
    mag = jnp.exp(dt * lr)
    ar, ai = mag * jnp.cos(dt * li), mag * jnp.sin(dt * li)
    den = lr * lr + li * li
    zr = ((ar - 1.0) * lr + ai * li) / den
    zi = (ai * lr - (ar - 1.0) * li) / den
    br, bi = b_re.astype(F32), b_im.astype(F32)
    bbr = zr[..., None] * br - zi[..., None] * bi
    bbi = zr[..., None] * bi + zi[..., None] * br
    cr, ci = c_re.astype(F32), c_im.astype(F32)
    eye_m = jnp.eye(S5_PAIRS, dtype=F32)
    eye_e = jnp.eye(2, dtype=F32)
    split = lambda t: t.reshape((S5_CB, S5_PAIRS, 2) + t.shape[1:])
    bb = jnp.stack([split(bbr), split(bbi)], axis=0)
    bt = jnp.einsum('rjmepc,mn,ef->jnfcmrep', bb, eye_m, eye_e)
    bt = bt.reshape(S5_CB, 256, S5_BLK_STATE)
    cc = jnp.stack([split(cr), -split(ci)], axis=0)
    ct = jnp.einsum('rjmecp,mn,ef->jmrepnfc', cc, eye_m, eye_e)
    ct = ct.reshape(S5_CB, S5_BLK_STATE, 256)
    lanes = lambda t: t.reshape(S5_CB, S5_PAIRS, 2 * C_STATE)
    avec = jnp.concatenate([lanes(ar), lanes(ai)], axis=1)
    return bt.astype(BF16), ct.astype(BF16), avec


def _rotary_tables(seq):
    half = B_HD // 2
    inv_freq = ROPE_THETA ** (-jnp.arange(half, dtype=F32) / half)
    ang = jnp.arange(seq, dtype=jnp.int32).astype(F32)[:, None] * inv_freq[None, :]
    cos, sin = jnp.cos(ang), jnp.sin(ang)
    return jnp.concatenate([cos, cos], axis=-1), jnp.concatenate([-sin, sin], axis=-1)


def kernel(x, mem, norm_w, mem_norm_w, ab_w_in, ab_w_out, hgrn_lb_logits, hgrn_out_norm_w, s5_lambda_re, s5_lambda_im, s5_log_dt, s5_b_re, s5_b_im, s5_c_re, s5_c_im, s5_d, s5_w_glu, xattn_wq, xattn_wkv, xattn_wo, ffn_w_in, ffn_w_out):
    bsz, seq, _ = x.shape
    n = bsz * seq
    depth = norm_w.shape[0]
    nw = lambda layer, i: norm_w[layer, i].astype(F32).reshape(1, D_MODEL)
    lb_table = jnp.cumsum(jax.nn.softmax(hgrn_lb_logits.astype(F32), axis=0), axis=0)
    cos_t, sin_t = _rotary_tables(seq)
    xs = x.reshape(n, D_MODEL)
    for layer in range(depth):
        j = layer // 2
        time_major = layer % 2 == 1
        if not time_major:
            qa, fa, ia, ga, qb, kb, vb = _proj_ab(
                xs, nw(layer, 0), ab_w_in[j].astype(BF16), cos_t, sin_t)
            oa = _hgrn(qa, fa, ia, ga, lb_table[layer], hgrn_out_norm_w[j].astype(F32), bsz, seq)
            ob = _dil_attn(qb, kb, vb, bsz, seq)
            xs = _out_proj(oa.reshape(n, A_WIDTH), ob.reshape(n, B_WIDTH),
                           ab_w_out[j].astype(BF16), nw(layer, 1), xs)
        else:
            bt, ct, avec = _s5_weights(s5_lambda_re[j], s5_lambda_im[j], s5_log_dt[j],
                                       s5_b_re[j], s5_b_im[j], s5_c_re[j], s5_c_im[j])
            xs = _s5_block(xs, nw(layer, 0), bt, ct, avec,
                           s5_d[j].astype(F32).reshape(1, D_MODEL), s5_w_glu[j].astype(BF16),
                           nw(layer, 1), bsz, seq)
        k, v = _kv_proj(mem, mem_norm_w[layer].astype(F32).reshape(1, D_MODEL),
                        xattn_wkv[layer].astype(BF16))
        xs = _xattn(xs, k, v, xattn_wq[layer].astype(BF16), xattn_wo[layer].astype(BF16),
                    nw(layer, 2), nw(layer, 3), bsz, seq, time_major)
        next_time_major = layer + 1 < depth and (layer + 1) % 2 == 1
        xs = _ffn(xs, nw(layer, 4), ffn_w_in[layer].astype(BF16), ffn_w_out[layer].astype(BF16),
                  nw(layer, 5), bsz, seq, time_major, next_time_major)
    return xs.reshape(bsz, seq, D_MODEL)
```

```python
import math

import jax
import jax.numpy as jnp
from jax import lax
from jax.experimental import pallas as pl
from jax.experimental.pallas import tpu as pltpu

F32 = jnp.float32
BF16 = jnp.bfloat16

D_MODEL = 1024
NORM_EPS = 1e-6
A_HEADS = 4
A_DK = 128
A_WIDTH = 512
A_CHUNK = 32
B_HEADS = 4
B_HD = 128
B_WIDTH = 512
B_SPAN = 128
B_DILS = (1, 4, 16)
DIL_UNROLL = 4
ROPE_THETA = 10000.0
IN_AB_WIDTH = 4 * A_WIDTH + 3 * B_WIDTH
C_GROUP = 16
C_GROUPS = 64
C_STATE = 64
C_MIN_NEG_RE = -1e-4
MEM_LEN = 256
X_HEADS = 4
X_HD = 256
D_FF = 2816

TM = 512
FF_CHUNK = 1408
VMEM_LIMIT = 56 * 1024 * 1024

NT_DIMS = (((1,), (1,)), ((), ()))
TN_DIMS = (((0,), (0,)), ((), ()))


def _cparams(n_axes):
    return pltpu.CompilerParams(
        dimension_semantics=("arbitrary",) * n_axes, vmem_limit_bytes=VMEM_LIMIT)


def _resident(shape):
    zeros = (0,) * len(shape)
    return pl.BlockSpec(shape, lambda *_: zeros, pipeline_mode=pl.Buffered(1))


def _rms(xf, w):
    return xf * lax.rsqrt(jnp.mean(xf * xf, axis=-1, keepdims=True) + NORM_EPS) * w


def _dot(a, b):
    return jnp.dot(a, b, preferred_element_type=F32)


def _proj_ab_kernel(x_ref, nw_ref, w_ref, cos_ref, sin_ref,
                    qa_ref, fa_ref, ia_ref, ga_ref, qb_ref, kb_ref, vb_ref):
    h = _rms(x_ref[...], nw_ref[...]).astype(BF16)
    cos = cos_ref[...]
    sin = sin_ref[...]
    outs = (qa_ref, fa_ref, ia_ref, ga_ref, qb_ref, kb_ref, vb_ref)
    for j, o_ref in enumerate(outs):
        z = _dot(h, w_ref[:, j * 512:(j + 1) * 512])
        for hd in range(4):
            zh = z[:, hd * 128:(hd + 1) * 128]
            if j in (4, 5):
                zh = zh * cos + pltpu.roll(zh, 64, 1) * sin
            if j == 4:
                zh = zh * (B_HD ** -0.5)
            o_ref[hd] = zh.astype(o_ref.dtype)


def _proj_ab(x2, nw, w_in, cos_t, sin_t):
    n = x2.shape[0]
    pos_blocks = cos_t.shape[0] // TM
    head_out = lambda dt: jax.ShapeDtypeStruct((4, n, 128), dt)
    head_spec = pl.BlockSpec((4, TM, 128), lambda i: (0, i, 0))
    return pl.pallas_call(
        _proj_ab_kernel,
        grid=(n // TM,),
        in_specs=[
            pl.BlockSpec((TM, D_MODEL), lambda i: (i, 0)),
            _resident((1, D_MODEL)),
            _resident((D_MODEL, IN_AB_WIDTH)),
            pl.BlockSpec((TM, 128), lambda i: (i % pos_blocks, 0)),
            pl.BlockSpec((TM, 128), lambda i: (i % pos_blocks, 0)),
        ],
        out_specs=[head_spec] * 7,
        out_shape=[head_out(BF16), head_out(F32), head_out(BF16), head_out(BF16),
                   head_out(BF16), head_out(BF16), head_out(BF16)],
        compiler_params=_cparams(1),
        name="proj_ab",
    )(x2, nw, w_in, cos_t, sin_t)


HG_BLK = 256
HG_UNROLL = 4


def _hgrn_kernel(q_ref, f_ref, i_ref, g_ref, lb_ref, onw_ref, o_ref,
                 qdec_s, dec_s, oacc_s, upd_s, state_s):
    seq = q_ref.shape[0]
    lb = lb_ref[...]
    ri = lax.broadcasted_iota(jnp.int32, (2 * HG_BLK, HG_BLK), 0)
    ci = lax.broadcasted_iota(jnp.int32, (2 * HG_BLK, HG_BLK), 1)
    rr = jnp.where(ri >= HG_BLK, ri - HG_BLK, ri)
    same = (rr // A_CHUNK) == (ci // A_CHUNK)
    lmat = jnp.where(same & ((ci <= rr) | (ri >= HG_BLK)), 1.0, 0.0).astype(BF16)
    causal = (same & (ci <= rr))[:HG_BLK]

    for blk in range(seq // HG_BLK):
        rows = pl.ds(blk * HG_BLK, HG_BLK)
        f = lb + (1.0 - lb) * jax.nn.sigmoid(f_ref[rows, :])
        log_f = jnp.log(f)
        k = 1.0 - f
        hi = log_f.astype(BF16)
        lo = (log_f - hi.astype(F32)).astype(BF16)
        bb = _dot(lmat, hi) + _dot(lmat, lo)
        b = bb[:HG_BLK]
        b_last = bb[HG_BLK:]
        q_dec = (q_ref[rows, :].astype(F32) * jnp.exp(b)).astype(BF16)
        k_inv = (k * jnp.exp(-b)).astype(BF16)
        k_end = (k * jnp.exp(b_last - b)).astype(BF16)
        qdec_s[rows, :] = q_dec
        dec_s[rows, :] = jnp.exp(b_last)
        scores = lax.dot_general(q_dec, k_inv, NT_DIMS, preferred_element_type=F32)
        scores = jnp.where(causal, scores, 0.0).astype(BF16)
        v_blk = i_ref[rows, :]
        oacc_s[rows, :] = _dot(scores, v_blk)
        for c in range(HG_BLK // A_CHUNK):
            crow = slice(c * A_CHUNK, (c + 1) * A_CHUNK)
            upd_s[blk * (HG_BLK // A_CHUNK) + c] = lax.dot_general(
                v_blk[crow], k_end[crow], TN_DIMS, preferred_element_type=F32)

    def advance(n, state_t):
        state_s[n] = state_t.astype(BF16)
        dec = dec_s[pl.ds(pl.multiple_of(n * A_CHUNK, A_CHUNK), 1), :]
        return dec * state_t + upd_s[n]

    lax.fori_loop(0, seq // A_CHUNK, advance, jnp.zeros((A_DK, A_DK), F32), unroll=HG_UNROLL)

    for blk in range(seq // HG_BLK):
        parts = []
        for c in range(HG_BLK // A_CHUNK):
            n = blk * (HG_BLK // A_CHUNK) + c
            parts.append(lax.dot_general(qdec_s[n * A_CHUNK:(n + 1) * A_CHUNK, :], state_s[n],
                                         NT_DIMS, preferred_element_type=F32))
        rows = pl.ds(blk * HG_BLK, HG_BLK)
        o = oacc_s[rows, :] + jnp.concatenate(parts, axis=0)
        o = o * lax.rsqrt(jnp.mean(o * o, axis=-1, keepdims=True) + NORM_EPS) * onw_ref[...]
        g = g_ref[rows, :].astype(F32)
        o_ref[rows, :] = (o * (g * jax.nn.sigmoid(g))).astype(o_ref.dtype)


def _hgrn(qa, fa, ia, ga, lb, onw, bsz, seq):
    view = lambda t: t.reshape(4, bsz, seq, 128)
    head_spec = pl.BlockSpec((None, None, seq, 128), lambda b, h: (h, b, 0, 0))
    vec_spec = pl.BlockSpec((None, 1, 128), lambda b, h: (h, 0, 0))
    return pl.pallas_call(
        _hgrn_kernel,
        grid=(bsz, A_HEADS),
        in_specs=[head_spec] * 4 + [vec_spec] * 2,
        out_specs=pl.BlockSpec((None, seq, 128), lambda b, h: (b, 0, h)),
        out_shape=jax.ShapeDtypeStruct((bsz, seq, A_WIDTH), BF16),
        scratch_shapes=[pltpu.VMEM((seq, 128), BF16),
                        pltpu.VMEM((seq, 128), F32), pltpu.VMEM((seq, 128), F32),
                        pltpu.VMEM((seq // A_CHUNK, A_DK, A_DK), F32),
                        pltpu.VMEM((seq // A_CHUNK, A_DK, A_DK), BF16)],
        compiler_params=_cparams(2),
        name="hgrn2",
    )(view(qa), view(fa), view(ia), view(ga), lb.reshape(4, 1, 128), onw.reshape(4, 1, 128))


def _dil_kernel(q1_ref, k1_ref, v1_ref, q4_ref, k4_ref, v4_ref, q16_ref, k16_ref, v16_ref,
                o_ref, acc_s, m_s, l_s):
    seq = q1_ref.shape[0]
    views = {1: (q1_ref, k1_ref, v1_ref), 4: (q4_ref, k4_ref, v4_ref),
             16: (q16_ref, k16_ref, v16_ref)}

    iq = lax.broadcasted_iota(jnp.int32, (B_SPAN, 2 * B_SPAN), 0)
    ik = lax.broadcasted_iota(jnp.int32, (B_SPAN, 2 * B_SPAN), 1)
    mask_first = (ik <= iq)[:, :B_SPAN]
    mask_win = (ik >= iq) & (ik <= iq + B_SPAN)

    def attend(q, kk, vv, q_rows, mask, first):
        s = lax.dot_general(q, kk, NT_DIMS, preferred_element_type=F32)
        s = jnp.where(mask, s, -jnp.inf)
        m_blk = jnp.max(s, axis=-1, keepdims=True)
        if first:
            m_new = jnp.broadcast_to(m_blk, (B_SPAN, 128))
        else:
            m_old = m_s[q_rows, :]
            m_new = jnp.maximum(m_old, m_blk)
        reps = s.shape[1] // 128
        p = jnp.exp(s - jnp.concatenate([m_new] * reps, axis=1))
        l_blk = jnp.sum(p, axis=-1, keepdims=True)
        pv = _dot(p.astype(BF16), vv)
        if first:
            l_s[q_rows, :] = jnp.broadcast_to(l_blk, (B_SPAN, 128))
            acc_s[q_rows, :] = pv
        else:
            alpha = jnp.exp(m_old - m_new)
            l_s[q_rows, :] = alpha * l_s[q_rows, :] + l_blk
            acc_s[q_rows, :] = alpha * acc_s[q_rows, :] + pv
        m_s[q_rows, :] = m_new

    for dil in B_DILS:
        q_ref, k_ref, v_ref = views[dil]
        blocks = seq // (dil * B_SPAN)
        for r in range(dil):
            lanes = slice(r * 128, (r + 1) * 128)
            for blk in range(blocks):
                k_rows = slice(max(blk - 1, 0) * B_SPAN, (blk + 1) * B_SPAN)
                q_rows = pl.ds(r + blk * dil * B_SPAN, B_SPAN,
                               stride=None if dil == 1 else dil)
                attend(q_ref[blk * B_SPAN:(blk + 1) * B_SPAN, lanes], k_ref[k_rows, lanes],
                       v_ref[k_rows, lanes], q_rows, mask_first if blk == 0 else mask_win,
                       dil == 1)

    o_ref[...] = (acc_s[...] / l_s[...]).astype(o_ref.dtype)


def _dil_attn(qb, kb, vb, bsz, seq):
    def spec(d):
        return pl.BlockSpec((None, None, seq // d, d * 128), lambda b, h: (h, b, 0, 0))
    view = lambda t, d: t.reshape(4, bsz, seq // d, d * 128)
    return pl.pallas_call(
        _dil_kernel,
        grid=(bsz, B_HEADS),
        in_specs=[spec(d) for d in B_DILS for _ in range(3)],
        out_specs=pl.BlockSpec((None, seq, 128), lambda b, h: (b, 0, h)),
        out_shape=jax.ShapeDtypeStruct((bsz, seq, B_WIDTH), BF16),
        scratch_shapes=[pltpu.VMEM((seq, 128), F32)] * 3,
        compiler_params=_cparams(2),
        name="dilated_attn",
    )(*[view(t, d) for d in B_DILS for t in (qb, kb, vb)])


def _out_proj_kernel(oa_ref, ob_ref, w_ref, nw_ref, x_ref, o_ref):
    y = _dot(oa_ref[...], w_ref[:A_WIDTH, :]) + _dot(ob_ref[...], w_ref[A_WIDTH:, :])
    o_ref[...] = x_ref[...] + _rms(y, nw_ref[...])


def _out_proj(oa, ob, w_out, nw, x2):
    n = x2.shape[0]
    row = lambda w: pl.BlockSpec((TM, w), lambda i: (i, 0))
    return pl.pallas_call(
        _out_proj_kernel,
        grid=(n // TM,),
        in_specs=[row(A_WIDTH), row(B_WIDTH), _resident((D_MODEL, D_MODEL)),
                  _resident((1, D_MODEL)), row(D_MODEL)],
        out_specs=row(D_MODEL),
        out_shape=jax.ShapeDtypeStruct((n, D_MODEL), F32),
        compiler_params=_cparams(1),
        name="out_proj",
    )(oa, ob, w_out, nw, x2)


def _kv_kernel(mem_ref, nw_ref, w_ref, k_ref, v_ref):
    h = _rms(mem_ref[...], nw_ref[...]).astype(BF16)
    k_ref[...] = _dot(h, w_ref[:, :D_MODEL]).astype(BF16)
    v_ref[...] = _dot(h, w_ref[:, D_MODEL:]).astype(BF16)


def _kv_proj(mem, nw, wkv):
    bsz = mem.shape[0]
    blk = pl.BlockSpec((None, MEM_LEN, D_MODEL), lambda b: (b, 0, 0))
    kv = jax.ShapeDtypeStruct((bsz, MEM_LEN, D_MODEL), BF16)
    return pl.pallas_call(
        _kv_kernel,
        grid=(bsz,),
        in_specs=[blk, _resident((1, D_MODEL)), _resident((D_MODEL, 2 * D_MODEL))],
        out_specs=[blk, blk],
        out_shape=[kv, kv],
        compiler_params=_cparams(1),
        name="xattn_kv",
    )(mem, nw, wkv)


def _xattn_kernel(x_ref, k_ref, v_ref, wq_ref, wo_ref, nwa_ref, nwb_ref, o_ref):
    x = x_ref[...]
    h = _rms(x, nwa_ref[...]).astype(BF16)
    q = _dot(h, wq_ref[...])
    heads = []
    for hd in range(X_HEADS):
        cols = slice(hd * X_HD, (hd + 1) * X_HD)
        qh = (q[:, cols] * (X_HD ** -0.5)).astype(BF16)
        s = lax.dot_general(qh, k_ref[:, cols], NT_DIMS, preferred_element_type=F32)
        p = jnp.exp(s - jnp.max(s, axis=-1, keepdims=True))
        l = jnp.sum(p, axis=-1, keepdims=True)
        heads.append((_dot(p.astype(BF16), v_ref[:, cols]) / l).astype(BF16))
    y = _dot(jnp.concatenate(heads, axis=1), wo_ref[...])
    o_ref[...] = x + _rms(y, nwb_ref[...])


def _row_spec(time_major, seq):
    if time_major:
        return pl.BlockSpec((TM, D_MODEL), lambda b, i: (i, b))
    tiles = seq // TM
    return pl.BlockSpec((TM, D_MODEL), lambda b, i: (b * tiles + i, 0))


def _stream_shape(time_major, bsz, seq):
    shape = (seq, bsz * D_MODEL) if time_major else (bsz * seq, D_MODEL)
    return jax.ShapeDtypeStruct(shape, F32)


def _xattn(xs, k, v, wq, wo, nwa, nwb, bsz, seq, time_major):
    row = _row_spec(time_major, seq)
    kvs = pl.BlockSpec((None, MEM_LEN, D_MODEL), lambda b, i: (b, 0, 0))
    return pl.pallas_call(
        _xattn_kernel,
        grid=(bsz, seq // TM),
        in_specs=[row, kvs, kvs, _resident((D_MODEL, D_MODEL)), _resident((D_MODEL, D_MODEL)),
                  _resident((1, D_MODEL)), _resident((1, D_MODEL))],
        out_specs=row,
        out_shape=_stream_shape(time_major, bsz, seq),
        compiler_params=_cparams(2),
        name="xattn",
    )(xs, k, v, wq, wo, nwa, nwb)


def _ffn_kernel(x_ref, nwa_ref, w_in_ref, w_out_ref, nwb_ref, o_ref):
    x = x_ref[...]
    h = _rms(x, nwa_ref[...]).astype(BF16)
    y = None
    for j in range(D_FF // FF_CHUNK):
        z1 = _dot(h, w_in_ref[:, j * FF_CHUNK:(j + 1) * FF_CHUNK])
        z2 = _dot(h, w_in_ref[:, D_FF + j * FF_CHUNK:D_FF + (j + 1) * FF_CHUNK])
        a = (z1 * jax.nn.sigmoid(z1) * z2).astype(BF16)
        part = _dot(a, w_out_ref[j * FF_CHUNK:(j + 1) * FF_CHUNK, :])
        y = part if y is None else y + part
    o_ref[...] = x + _rms(y, nwb_ref[...])


def _ffn(xs, nwa, w_in, w_out, nwb, bsz, seq, in_time_major, out_time_major):
    return pl.pallas_call(
        _ffn_kernel,
        grid=(bsz, seq // TM),
        in_specs=[_row_spec(in_time_major, seq), _resident((1, D_MODEL)),
                  _resident((D_MODEL, 2 * D_FF)), _resident((D_FF, D_MODEL)),
                  _resident((1, D_MODEL))],
        out_specs=_row_spec(out_time_major, seq),
        out_shape=_stream_shape(out_time_major, bsz, seq),
        compiler_params=_cparams(2),
        name="ffn",
    )(xs, nwa, w_in, w_out, nwb)


S5_NB = 8
S5_TT = 64
S5_CB = 4
S5_BLK_STATE = 16 * 2 * C_STATE
S5_PAIRS = 8


def _s5_kernel(x_ref, nwa_ref, bt_ref, ct_ref, a_ref, d_ref, wg_ref, nwb_ref, o_ref,
               xs_s, v_s, h_s, y_s, st_s, os_s):
    tt = x_ref.shape[0]
    rows = tt * S5_NB
    lane_slabs = D_MODEL // 128

    @pl.when(pl.program_id(1) == 0)
    def _():
        st_s[...] = jnp.zeros_like(st_s)

    for b in range(S5_NB):
        for k in range(lane_slabs):
            lanes = slice(b * D_MODEL + k * 128, b * D_MODEL + (k + 1) * 128)
            xs_s[pl.ds(k * rows + b, tt, stride=S5_NB), :] = x_ref[:, lanes]

    def load_x():
        return jnp.concatenate([xs_s[k * rows:(k + 1) * rows, :] for k in range(lane_slabs)],
                               axis=1)

    ub = _rms(load_x(), nwa_ref[...]).astype(BF16)

    for cb in range(S5_CB):
        vb, hb = v_s.at[cb % 2], h_s.at[cb % 2]
        vb[...] = _dot(ub[:, cb * 256:(cb + 1) * 256], bt_ref[cb])
        for m in range(S5_PAIRS):
            ar = jnp.broadcast_to(a_ref[cb, m:m + 1, :], (S5_NB, 128))
            ai = jnp.broadcast_to(a_ref[cb, S5_PAIRS + m:S5_PAIRS + m + 1, :], (S5_NB, 128))
            c_re = slice(256 * m, 256 * m + 128)
            c_im = slice(256 * m + 128, 256 * m + 256)
            s_re = slice(cb * S5_BLK_STATE + 256 * m, cb * S5_BLK_STATE + 256 * m + 128)
            s_im = slice(cb * S5_BLK_STATE + 256 * m + 128, cb * S5_BLK_STATE + 256 * m + 256)
            hr, hi = st_s[:, s_re], st_s[:, s_im]
            for i in range(tt // 2):
                hist_r, hist_i = [], []
                for step in range(2):
                    rows_t = slice((2 * i + step) * S5_NB, (2 * i + step + 1) * S5_NB)
                    hr, hi = (ar * hr - ai * hi + vb[rows_t, c_re],
                              ar * hi + ai * hr + vb[rows_t, c_im])
                    hist_r.append(hr)
                    hist_i.append(hi)
                both = slice(2 * i * S5_NB, (2 * i + 2) * S5_NB)
                hb[both, c_re] = jnp.concatenate(hist_r, axis=0).astype(BF16)
                hb[both, c_im] = jnp.concatenate(hist_i, axis=0).astype(BF16)
            st_s[:, s_re] = hr
            st_s[:, s_im] = hi
        y_s[:, cb * 256:(cb + 1) * 256] = _dot(hb[...], ct_ref[cb])

    x = load_x()
    t = y_s[...] + d_ref[...] * _rms(x, nwa_ref[...])
    gl = (0.5 * t * (1.0 + lax.erf(t * math.sqrt(0.5)))).astype(BF16)
    z = _dot(gl, wg_ref[...])
    y = z[:, :D_MODEL] * jax.nn.sigmoid(z[:, D_MODEL:])
    out = x + _rms(y, nwb_ref[...])
    for k in range(lane_slabs):
        os_s[k * rows:(k + 1) * rows, :] = out[:, k * 128:(k + 1) * 128]
    for b in range(S5_NB):
        for k in range(lane_slabs):
            lanes = slice(b * D_MODEL + k * 128, b * D_MODEL + (k + 1) * 128)
            o_ref[:, lanes] = os_s[pl.ds(k * rows + b, tt, stride=S5_NB), :]


def _s5_block(xs, nwa, bt, ct, avec, d, w_glu, nwb, bsz, seq, tt=S5_TT):
    rows = tt * S5_NB
    tile = pl.BlockSpec((tt, S5_NB * D_MODEL), lambda g, i: (i, g))
    return pl.pallas_call(
        _s5_kernel,
        grid=(bsz // S5_NB, seq // tt),
        in_specs=[tile, _resident((1, D_MODEL)), _resident(bt.shape), _resident(ct.shape),
                  _resident(avec.shape), _resident((1, D_MODEL)),
                  _resident((D_MODEL, 2 * D_MODEL)), _resident((1, D_MODEL))],
        out_specs=tile,
        out_shape=_stream_shape(True, bsz, seq),
        scratch_shapes=[pltpu.VMEM((D_MODEL // 128 * rows, 128), F32),
                        pltpu.VMEM((2, rows, S5_BLK_STATE), F32),
                        pltpu.VMEM((2, rows, S5_BLK_STATE), BF16),
                        pltpu.VMEM((rows, D_MODEL), F32),
                        pltpu.VMEM((S5_NB, S5_CB * S5_BLK_STATE), F32),
                        pltpu.VMEM((D_MODEL // 128 * rows, 128), F32)],
        compiler_params=_cparams(2),
        name="s5_block",
    )(xs, nwa, bt, ct, avec, d, w_glu, nwb)


def _s5_weights(lam_re, lam_im, log_dt, b_re, b_im, c_re, c_im):
    lr = jnp.minimum(lam_re.astype(F32), C_MIN_NEG_RE)
    li = lam_im.astype(F32)
    dt = jnp.exp(log_dt.astype(F32))[:, None]
    mag = jnp.exp(dt * lr)
    ar, ai = mag * jnp.cos(dt * li), mag * jnp.sin(dt * li)
    den = lr * lr + li * li
    zr = ((ar - 1.0) * lr + ai * li) / den
    zi = (ai * lr - (ar - 1.0) * li) / den
    br, bi = b_re.astype(F32), b_im.astype(F32)
    bbr = zr[..., None] * br - zi[..., None] * bi
    bbi = zr[..., None] * bi + zi[..., None] * br
    cr, ci = c_re.astype(F32), c_im.astype(F32)
    eye_m = jnp.eye(S5_PAIRS, dtype=F32)
    eye_e = jnp.eye(2, dtype=F32)
    split = lambda t: t.reshape((S5_CB, S5_PAIRS, 2) + t.shape[1:])
    bb = jnp.stack([split(bbr), split(bbi)], axis=0)
    bt = jnp.einsum('rjmepc,mn,ef->jnfcmrep', bb, eye_m, eye_e)
    bt = bt.reshape(S5_CB, 256, S5_BLK_STATE)
    cc = jnp.stack([split(cr), -split(ci)], axis=0)
    ct = jnp.einsum('rjmecp,mn,ef->jmrepnfc', cc, eye_m, eye_e)
    ct = ct.reshape(S5_CB, S5_BLK_STATE, 256)
    lanes = lambda t: t.reshape(S5_CB, S5_PAIRS, 2 * C_STATE)
    avec = jnp.concatenate([lanes(ar), lanes(ai)], axis=1)
    return bt.astype(BF16), ct.astype(BF16), avec


def _rotary_tables(seq):
    half = B_HD // 2
    inv_freq = ROPE_THETA ** (-jnp.arange(half, dtype=F32) / half)
    ang = jnp.arange(seq, dtype=jnp.int32).astype(F32)[:, None] * inv_freq[None, :]
    cos, sin = jnp.cos(ang), jnp.sin(ang)
    return jnp.concatenate([cos, cos], axis=-1), jnp.concatenate([-sin, sin], axis=-1)


def kernel(x, mem, norm_w, mem_norm_w, ab_w_in, ab_w_out, hgrn_lb_logits, hgrn_out_norm_w, s5_lambda_re, s5_lambda_im, s5_log_dt, s5_b_re, s5_b_im, s5_c_re, s5_c_im, s5_d, s5_w_glu, xattn_wq, xattn_wkv, xattn_wo, ffn_w_in, ffn_w_out):
    bsz, seq, _ = x.shape
    n = bsz * seq
    depth = norm_w.shape[0]
    nw = lambda layer, i: norm_w[layer, i].astype(F32).reshape(1, D_MODEL)
    lb_table = jnp.cumsum(jax.nn.softmax(hgrn_lb_logits.astype(F32), axis=0), axis=0)
    cos_t, sin_t = _rotary_tables(seq)
    xs = x.reshape(n, D_MODEL)
    for layer in range(depth):
        j = layer // 2
        time_major = layer % 2 == 1
        if not time_major:
            qa, fa, ia, ga, qb, kb, vb = _proj_ab(
                xs, nw(layer, 0), ab_w_in[j].astype(BF16), cos_t, sin_t)
            oa = _hgrn(qa, fa, ia, ga, lb_table[layer], hgrn_out_norm_w[j].astype(F32), bsz, seq)
            ob = _dil_attn(qb, kb, vb, bsz, seq)
            xs = _out_proj(oa.reshape(n, A_WIDTH), ob.reshape(n, B_WIDTH),
                           ab_w_out[j].astype(BF16), nw(layer, 1), xs)
        else:
            bt, ct, avec = _s5_weights(s5_lambda_re[j], s5_lambda_im[j], s5_log_dt[j],
                                       s5_b_re[j], s5_b_im[j], s5_c_re[j], s5_c_im[j])
            xs = _s5_block(xs, nw(layer, 0), bt, ct, avec,
                           s5_d[j].astype(F32).reshape(1, D_MODEL), s5_w_glu[j].astype(BF16),
                           nw(layer, 1), bsz, seq)
        k, v = _kv_proj(mem, mem_norm_w[layer].astype(F32).reshape(1, D_MODEL),
                        xattn_wkv[layer].astype(BF16))
        xs = _xattn(xs, k, v, xattn_wq[layer].astype(BF16), xattn_wo[layer].astype(BF16),
                    nw(layer, 2), nw(layer, 3), bsz, seq, time_major)
        next_time_major = layer + 1 < depth and (layer + 1) % 2 == 1
        xs = _ffn(xs, nw(layer, 4), ffn_w_in[layer].astype(BF16), ffn_w_out[layer].astype(BF16),
                  nw(layer, 5), bsz, seq, time_major, next_time_major)
    return xs.reshape(bsz, seq, D_MODEL)
```

```python
import functools
import math

import jax
import jax.numpy as jnp
from jax import lax
from jax.experimental import pallas as pl
from jax.experimental.pallas import tpu as pltpu

F32 = jnp.float32
BF16 = jnp.bfloat16

D_MODEL = 1024
NORM_EPS = 1e-6
A_HEADS = 4
A_DK = 128
A_WIDTH = 512
A_CHUNK = 32
B_HEADS = 4
B_HD = 128
B_WIDTH = 512
B_SPAN = 128
B_DILS = (1, 4, 16)
ROPE_THETA = 10000.0
IN_AB_WIDTH = 4 * A_WIDTH + 3 * B_WIDTH
C_GROUP = 16
C_GROUPS = 64
C_STATE = 64
C_MIN_NEG_RE = -1e-4
MEM_LEN = 256
X_HEADS = 4
X_HD = 256
D_FF = 2816

TM = 512
FF_CHUNK = 1408
VMEM_LIMIT = 56 * 1024 * 1024

NT_DIMS = (((1,), (1,)), ((), ()))
TN_DIMS = (((0,), (0,)), ((), ()))


def _cparams(n_axes):
    return pltpu.CompilerParams(
        dimension_semantics=("arbitrary",) * n_axes, vmem_limit_bytes=VMEM_LIMIT)


def _resident(shape):
    zeros = (0,) * len(shape)
    return pl.BlockSpec(shape, lambda *_: zeros, pipeline_mode=pl.Buffered(1))


def _rms(xf, w):
    return xf * lax.rsqrt(jnp.mean(xf * xf, axis=-1, keepdims=True) + NORM_EPS) * w


def _dot(a, b):
    return jnp.dot(a, b, preferred_element_type=F32)


def _proj_ab_kernel(x_ref, nw_ref, w_ref, cos_ref, sin_ref,
                    qa_ref, fa_ref, ia_ref, ga_ref, *b_refs_and_scratch):
    b_refs, stage_s = b_refs_and_scratch[:-1], b_refs_and_scratch[-1]
    tm = x_ref.shape[0]
    h = _rms(x_ref[...], nw_ref[...]).astype(BF16)
    cos = cos_ref[...]
    sin = sin_ref[...]
    for j, o_ref in enumerate((qa_ref, fa_ref, ia_ref, ga_ref)):
        z = _dot(h, w_ref[:, j * 512:(j + 1) * 512])
        for hd in range(4):
            o_ref[hd] = z[:, hd * 128:(hd + 1) * 128].astype(o_ref.dtype)
    for j in range(3):
        z = _dot(h, w_ref[:, (4 + j) * 512:(5 + j) * 512])
        for hd in range(4):
            zh = z[:, hd * 128:(hd + 1) * 128]
            if j < 2:
                zh = zh * cos + pltpu.roll(zh, 64, 1) * sin
            if j == 0:
                zh = zh * (B_HD ** -0.5)
            base = (j * 4 + hd) * tm
            stage_s[base:base + tm, :] = zh
            for d_idx, dil in enumerate(B_DILS):
                o_ref = b_refs[3 * d_idx + j]
                for r in range(dil):
                    rows = pl.ds(base + r, tm // dil, stride=None if dil == 1 else dil)
                    o_ref[hd, :, r * 128:(r + 1) * 128] = stage_s[rows, :].astype(o_ref.dtype)


def _proj_ab(x2, nw, w_in, cos_t, sin_t, bsz, seq):
    n = x2.shape[0]
    tiles = seq // TM
    head_out = lambda dt: jax.ShapeDtypeStruct((4, n, 128), dt)
    head_spec = pl.BlockSpec((4, TM, 128), lambda i: (0, i, 0))
    view_out = lambda d: jax.ShapeDtypeStruct((4, bsz, seq // d, d * 128), BF16)
    view_spec = lambda d: pl.BlockSpec((4, None, TM // d, d * 128),
                                       lambda i: (0, i // tiles, i % tiles, 0))
    return pl.pallas_call(
        _proj_ab_kernel,
        grid=(n // TM,),
        in_specs=[
            pl.BlockSpec((TM, D_MODEL), lambda i: (i, 0)),
            _resident((1, D_MODEL)),
            _resident((D_MODEL, IN_AB_WIDTH)),
            pl.BlockSpec((TM, 128), lambda i: (i % tiles, 0)),
            pl.BlockSpec((TM, 128), lambda i: (i % tiles, 0)),
        ],
        out_specs=[head_spec] * 4 + [view_spec(d) for d in B_DILS for _ in range(3)],
        out_shape=([head_out(BF16), head_out(F32), head_out(BF16), head_out(BF16)]
                   + [view_out(d) for d in B_DILS for _ in range(3)]),
        scratch_shapes=[pltpu.VMEM((12 * TM, 128), F32)],
        compiler_params=_cparams(1),
        name="proj_ab",
    )(x2, nw, w_in, cos_t, sin_t)


HG_BLK = 256
HG_UNROLL = 4


def _hgrn_kernel(q_ref, f_ref, i_ref, g_ref, lb_ref, onw_ref, o_ref,
                 qdec_s, dec_s, oacc_s, upd_s, state_s):
    seq = q_ref.shape[0]
    lb = lb_ref[...]
    ri = lax.broadcasted_iota(jnp.int32, (2 * HG_BLK, HG_BLK), 0)
    ci = lax.broadcasted_iota(jnp.int32, (2 * HG_BLK, HG_BLK), 1)
    rr = jnp.where(ri >= HG_BLK, ri - HG_BLK, ri)
    same = (rr // A_CHUNK) == (ci // A_CHUNK)
    lmat = jnp.where(same & ((ci <= rr) | (ri >= HG_BLK)), 1.0, 0.0).astype(BF16)
    causal = (same & (ci <= rr))[:HG_BLK]

    for blk in range(seq // HG_BLK):
        rows = pl.ds(blk * HG_BLK, HG_BLK)
        f = lb + (1.0 - lb) * jax.nn.sigmoid(f_ref[rows, :])
        log_f = jnp.log(f)
        k = 1.0 - f
        hi = log_f.astype(BF16)
        lo = (log_f - hi.astype(F32)).astype(BF16)
        bb = _dot(lmat, hi) + _dot(lmat, lo)
        b = bb[:HG_BLK]
        b_last = bb[HG_BLK:]
        q_dec = (q_ref[rows, :].astype(F32) * jnp.exp(b)).astype(BF16)
        k_inv = (k * jnp.exp(-b)).astype(BF16)
        k_end = (k * jnp.exp(b_last - b)).astype(BF16)
        qdec_s[rows, :] = q_dec
        dec_s[rows, :] = jnp.exp(b_last)
        scores = lax.dot_general(q_dec, k_inv, NT_DIMS, preferred_element_type=F32)
        scores = jnp.where(causal, scores, 0.0).astype(BF16)
        v_blk = i_ref[rows, :]
        oacc_s[rows, :] = _dot(scores, v_blk)
        for c in range(HG_BLK // A_CHUNK):
            crow = slice(c * A_CHUNK, (c + 1) * A_CHUNK)
            upd_s[blk * (HG_BLK // A_CHUNK) + c] = lax.dot_general(
                v_blk[crow], k_end[crow], TN_DIMS, preferred_element_type=F32)

    def advance(n, state_t):
        state_s[n] = state_t.astype(BF16)
        dec = dec_s[pl.ds(pl.multiple_of(n * A_CHUNK, A_CHUNK), 1), :]
        return dec * state_t + upd_s[n]

    lax.fori_loop(0, seq // A_CHUNK, advance, jnp.zeros((A_DK, A_DK), F32), unroll=HG_UNROLL)

    for blk in range(seq // HG_BLK):
        parts = []
        for c in range(HG_BLK // A_CHUNK):
            n = blk * (HG_BLK // A_CHUNK) + c
            parts.append(lax.dot_general(qdec_s[n * A_CHUNK:(n + 1) * A_CHUNK, :], state_s[n],
                                         NT_DIMS, preferred_element_type=F32))
        rows = pl.ds(blk * HG_BLK, HG_BLK)
        o = oacc_s[rows, :] + jnp.concatenate(parts, axis=0)
        o = o * lax.rsqrt(jnp.mean(o * o, axis=-1, keepdims=True) + NORM_EPS) * onw_ref[...]
        g = g_ref[rows, :].astype(F32)
        o_ref[rows, :] = (o * (g * jax.nn.sigmoid(g))).astype(o_ref.dtype)


def _hgrn(qa, fa, ia, ga, lb, onw, bsz, seq):
    view = lambda t: t.reshape(4, bsz, seq, 128)
    head_spec = pl.BlockSpec((None, None, seq, 128), lambda b, h: (h, b, 0, 0))
    vec_spec = pl.BlockSpec((None, 1, 128), lambda b, h: (h, 0, 0))
    return pl.pallas_call(
        _hgrn_kernel,
        grid=(bsz, A_HEADS),
        in_specs=[head_spec] * 4 + [vec_spec] * 2,
        out_specs=pl.BlockSpec((None, seq, 128), lambda b, h: (b, 0, h)),
        out_shape=jax.ShapeDtypeStruct((bsz, seq, A_WIDTH), BF16),
        scratch_shapes=[pltpu.VMEM((seq, 128), BF16),
                        pltpu.VMEM((seq, 128), F32), pltpu.VMEM((seq, 128), F32),
                        pltpu.VMEM((seq // A_CHUNK, A_DK, A_DK), F32),
                        pltpu.VMEM((seq // A_CHUNK, A_DK, A_DK), BF16)],
        compiler_params=_cparams(2),
        name="hgrn2",
    )(view(qa), view(fa), view(ia), view(ga), lb.reshape(4, 1, 128), onw.reshape(4, 1, 128))


def _dil_kernel(q1_ref, k1_ref, v1_ref, q4_ref, k4_ref, v4_ref, q16_ref, k16_ref, v16_ref,
                o_ref, acc_s, m_s, l_s):
    seq = q1_ref.shape[0]
    views = {1: (q1_ref, k1_ref, v1_ref), 4: (q4_ref, k4_ref, v4_ref),
             16: (q16_ref, k16_ref, v16_ref)}

    iq = lax.broadcasted_iota(jnp.int32, (B_SPAN, 2 * B_SPAN), 0)
    ik = lax.broadcasted_iota(jnp.int32, (B_SPAN, 2 * B_SPAN), 1)
    mask_first = (ik <= iq)[:, :B_SPAN]
    mask_win = (ik >= iq) & (ik <= iq + B_SPAN)

    def attend(q, kk, vv, q_rows, mask, first):
        s = lax.dot_general(q, kk, NT_DIMS, preferred_element_type=F32)
        s = jnp.where(mask, s, -jnp.inf)
        m_blk = jnp.max(s, axis=-1, keepdims=True)
        if first:
            m_new = jnp.broadcast_to(m_blk, (B_SPAN, 128))
        else:
            m_old = m_s[q_rows, :]
            m_new = jnp.maximum(m_old, m_blk)
        reps = s.shape[1] // 128
        p = jnp.exp(s - jnp.concatenate([m_new] * reps, axis=1))
        l_blk = jnp.sum(p, axis=-1, keepdims=True)
        pv = _dot(p.astype(BF16), vv)
        if first:
            l_s[q_rows, :] = jnp.broadcast_to(l_blk, (B_SPAN, 128))
            acc_s[q_rows, :] = pv
        else:
            alpha = jnp.exp(m_old - m_new)
            l_s[q_rows, :] = alpha * l_s[q_rows, :] + l_blk
            acc_s[q_rows, :] = alpha * acc_s[q_rows, :] + pv
        m_s[q_rows, :] = m_new

    for dil in B_DILS:
        q_ref, k_ref, v_ref = views[dil]
        blocks = seq // (dil * B_SPAN)
        for r in range(dil):
            lanes = slice(r * 128, (r + 1) * 128)
            for blk in range(blocks):
                k_rows = slice(max(blk - 1, 0) * B_SPAN, (blk + 1) * B_SPAN)
                q_rows = pl.ds(r + blk * dil * B_SPAN, B_SPAN,
                               stride=None if dil == 1 else dil)
                attend(q_ref[blk * B_SPAN:(blk + 1) * B_SPAN, lanes], k_ref[k_rows, lanes],
                       v_ref[k_rows, lanes], q_rows, mask_first if blk == 0 else mask_win,
                       dil == 1)

    o_ref[...] = (acc_s[...] / l_s[...]).astype(o_ref.dtype)


def _dil_attn(qkv_views, bsz, seq):
    def spec(d):
        return pl.BlockSpec((None, None, seq // d, d * 128), lambda b, h: (h, b, 0, 0))
    return pl.pallas_call(
        _dil_kernel,
        grid=(bsz, B_HEADS),
        in_specs=[spec(d) for d in B_DILS for _ in range(3)],
        out_specs=pl.BlockSpec((None, seq, 128), lambda b, h: (b, 0, h)),
        out_shape=jax.ShapeDtypeStruct((bsz, seq, B_WIDTH), BF16),
        scratch_shapes=[pltpu.VMEM((seq, 128), F32)] * 3,
        compiler_params=_cparams(2),
        name="dilated_attn",
    )(*qkv_views)


def _kv_kernel(mem_ref, nw_ref, w_ref, k_ref, v_ref):
    h = _rms(mem_ref[...], nw_ref[...]).astype(BF16)
    k_ref[...] = _dot(h, w_ref[:, :D_MODEL]).astype(BF16)
    v_ref[...] = _dot(h, w_ref[:, D_MODEL:]).astype(BF16)


def _kv_proj(mem, nw, wkv):
    bsz = mem.shape[0]
    blk = pl.BlockSpec((None, MEM_LEN, D_MODEL), lambda b: (b, 0, 0))
    kv = jax.ShapeDtypeStruct((bsz, MEM_LEN, D_MODEL), BF16)
    return pl.pallas_call(
        _kv_kernel,
        grid=(bsz,),
        in_specs=[blk, _resident((1, D_MODEL)), _resident((D_MODEL, 2 * D_MODEL))],
        out_specs=[blk, blk],
        out_shape=[kv, kv],
        compiler_params=_cparams(1),
        name="xattn_kv",
    )(mem, nw, wkv)


def _xattn_rows(x, k_ref, v_ref, wq_ref, wo_ref, nwa_ref, nwb_ref):
    h = _rms(x, nwa_ref[...]).astype(BF16)
    q = _dot(h, wq_ref[...])
    heads = []
    for hd in range(X_HEADS):
        cols = slice(hd * X_HD, (hd + 1) * X_HD)
        qh = (q[:, cols] * (X_HD ** -0.5)).astype(BF16)
        s = lax.dot_general(qh, k_ref[:, cols], NT_DIMS, preferred_element_type=F32)
        p = jnp.exp(s - jnp.max(s, axis=-1, keepdims=True))
        l = jnp.sum(p, axis=-1, keepdims=True)
        heads.append((_dot(p.astype(BF16), v_ref[:, cols]) / l).astype(BF16))
    y = _dot(jnp.concatenate(heads, axis=1), wo_ref[...])
    return x + _rms(y, nwb_ref[...])


def _ffn_rows(x, nwa_ref, w_in_ref, w_out_ref, nwb_ref):
    h = _rms(x, nwa_ref[...]).astype(BF16)
    y = None
    for j in range(D_FF // FF_CHUNK):
        z1 = _dot(h, w_in_ref[:, j * FF_CHUNK:(j + 1) * FF_CHUNK])
        z2 = _dot(h, w_in_ref[:, D_FF + j * FF_CHUNK:D_FF + (j + 1) * FF_CHUNK])
        a = (z1 * jax.nn.sigmoid(z1) * z2).astype(BF16)
        part = _dot(a, w_out_ref[j * FF_CHUNK:(j + 1) * FF_CHUNK, :])
        y = part if y is None else y + part
    return x + _rms(y, nwb_ref[...])


def _tail_kernel(*refs, with_mixer_proj):
    if with_mixer_proj:
        oa_ref, ob_ref, w_mix_ref, nw_mix_ref = refs[:4]
        refs = refs[4:]
    (x_ref, k_ref, v_ref, wq_ref, wo_ref, nw2_ref, nw3_ref,
     nw4_ref, w_in_ref, w_out_ref, nw5_ref, o_ref) = refs
    x = x_ref[...]
    if with_mixer_proj:
        y = (_dot(oa_ref[...], w_mix_ref[:A_WIDTH, :]) + _dot(ob_ref[...], w_mix_ref[A_WIDTH:, :]))
        x = x + _rms(y, nw_mix_ref[...])
    x = _xattn_rows(x, k_ref, v_ref, wq_ref, wo_ref, nw2_ref, nw3_ref)
    o_ref[...] = _ffn_rows(x, nw4_ref, w_in_ref, w_out_ref, nw5_ref)


def _row_spec(time_major, seq):
    if time_major:
        return pl.BlockSpec((TM, D_MODEL), lambda b, i: (i, b))
    tiles = seq // TM
    return pl.BlockSpec((TM, D_MODEL), lambda b, i: (b * tiles + i, 0))


def _stream_shape(time_major, bsz, seq):
    shape = (seq, bsz * D_MODEL) if time_major else (bsz * seq, D_MODEL)
    return jax.ShapeDtypeStruct(shape, F32)


def _layer_tail(xs, mixer, k, v, wq, wo, nw2, nw3, nw4, w_in, w_out, nw5, bsz, seq,
                in_time_major, out_time_major):
    tiles = seq // TM
    vec = _resident((1, D_MODEL))
    kvs = pl.BlockSpec((None, MEM_LEN, D_MODEL), lambda b, i: (b, 0, 0))
    in_specs = [_row_spec(in_time_major, seq), kvs, kvs,
                _resident((D_MODEL, D_MODEL)), _resident((D_MODEL, D_MODEL)), vec, vec,
                vec, _resident((D_MODEL, 2 * D_FF)), _resident((D_FF, D_MODEL)), vec]
    operands = [xs, k, v, wq, wo, nw2, nw3, nw4, w_in, w_out, nw5]
    if mixer is not None:
        half = pl.BlockSpec((TM, A_WIDTH), lambda b, i: (b * tiles + i, 0))
        in_specs = [half, half, _resident((D_MODEL, D_MODEL)), vec] + in_specs
        operands = list(mixer) + operands
    return pl.pallas_call(
        functools.partial(_tail_kernel, with_mixer_proj=mixer is not None),
        grid=(bsz, tiles),
        in_specs=in_specs,
        out_specs=_row_spec(out_time_major, seq),
        out_shape=_stream_shape(out_time_major, bsz, seq),
        compiler_params=_cparams(2),
        name="layer_tail",
    )(*operands)


S5_NB = 8
S5_TT = 64
S5_CB = 4
S5_BLK_STATE = 16 * 2 * C_STATE
S5_PAIRS = 8


def _s5_kernel(x_ref, nwa_ref, bt_ref, ct_ref, a_ref, d_ref, wg_ref, nwb_ref, o_ref,
               xs_s, v_s, h_s, y_s, st_s, os_s):
    tt = x_ref.shape[0]
    rows = tt * S5_NB
    lane_slabs = D_MODEL // 128

    @pl.when(pl.program_id(1) == 0)
    def _():
        st_s[...] = jnp.zeros_like(st_s)

    for b in range(S5_NB):
        for k in range(lane_slabs):
            lanes = slice(b * D_MODEL + k * 128, b * D_MODEL + (k + 1) * 128)
            xs_s[pl.ds(k * rows + b, tt, stride=S5_NB), :] = x_ref[:, lanes]

    def load_x():
        return jnp.concatenate([xs_s[k * rows:(k + 1) * rows, :] for k in range(lane_slabs)],
                               axis=1)

    ub = _rms(load_x(), nwa_ref[...]).astype(BF16)

    for cb in range(S5_CB):
        vb, hb = v_s.at[cb % 2], h_s.at[cb % 2]
        vb[...] = _dot(ub[:, cb * 256:(cb + 1) * 256], bt_ref[cb])
        for m in range(S5_PAIRS):
            ar = jnp.broadcast_to(a_ref[cb, m:m + 1, :], (S5_NB, 128))
            ai = jnp.broadcast_to(a_ref[cb, S5_PAIRS + m:S5_PAIRS + m + 1, :], (S5_NB, 128))
            c_re = slice(256 * m, 256 * m + 128)
            c_im = slice(256 * m + 128, 256 * m + 256)
            s_re = slice(cb * S5_BLK_STATE + 256 * m, cb * S5_BLK_STATE + 256 * m + 128)
            s_im = slice(cb * S5_BLK_STATE + 256 * m + 128, cb * S5_BLK_STATE + 256 * m + 256)
            hr, hi = st_s[:, s_re], st_s[:, s_im]
            for i in range(tt // 2):
                hist_r, hist_i = [], []
                for step in range(2):
                    rows_t = slice((2 * i + step) * S5_NB, (2 * i + step + 1) * S5_NB)
                    hr, hi = (ar * hr - ai * hi + vb[rows_t, c_re],
                              ar * hi + ai * hr + vb[rows_t, c_im])
                    hist_r.append(hr)
                    hist_i.append(hi)
                both = slice(2 * i * S5_NB, (2 * i + 2) * S5_NB)
                hb[both, c_re] = jnp.concatenate(hist_r, axis=0).astype(BF16)
                hb[both, c_im] = jnp.concatenate(hist_i, axis=0).astype(BF16)
            st_s[:, s_re] = hr
            st_s[:, s_im] = hi
        y_s[:, cb * 256:(cb + 1) * 256] = _dot(hb[...], ct_ref[cb])

    x = load_x()
    t = y_s[...] + d_ref[...] * _rms(x, nwa_ref[...])
    gl = (0.5 * t * (1.0 + lax.erf(t * math.sqrt(0.5)))).astype(BF16)
    z = _dot(gl, wg_ref[...])
    y = z[:, :D_MODEL] * jax.nn.sigmoid(z[:, D_MODEL:])
    out = x + _rms(y, nwb_ref[...])
    for k in range(lane_slabs):
        os_s[k * rows:(k + 1) * rows, :] = out[:, k * 128:(k + 1) * 128]
    for b in range(S5_NB):
        for k in range(lane_slabs):
            lanes = slice(b * D_MODEL + k * 128, b * D_MODEL + (k + 1) * 128)
            o_ref[:, lanes] = os_s[pl.ds(k * rows + b, tt, stride=S5_NB), :]


def _s5_block(xs, nwa, bt, ct, avec, d, w_glu, nwb, bsz, seq, tt=S5_TT):
    rows = tt * S5_NB
    tile = pl.BlockSpec((tt, S5_NB * D_MODEL), lambda g, i: (i, g))
    return pl.pallas_call(
        _s5_kernel,
        grid=(bsz // S5_NB, seq // tt),
        in_specs=[tile, _resident((1, D_MODEL)), _resident(bt.shape), _resident(ct.shape),
                  _resident(avec.shape), _resident((1, D_MODEL)),
                  _resident((D_MODEL, 2 * D_MODEL)), _resident((1, D_MODEL))],
        out_specs=tile,
        out_shape=_stream_shape(True, bsz, seq),
        scratch_shapes=[pltpu.VMEM((D_MODEL // 128 * rows, 128), F32),
                        pltpu.VMEM((2, rows, S5_BLK_STATE), F32),
                        pltpu.VMEM((2, rows, S5_BLK_STATE), BF16),
                        pltpu.VMEM((rows, D_MODEL), F32),
                        pltpu.VMEM((S5_NB, S5_CB * S5_BLK_STATE), F32),
                        pltpu.VMEM((D_MODEL // 128 * rows, 128), F32)],
        compiler_params=_cparams(2),
        name="s5_block",
    )(xs, nwa, bt, ct, avec, d, w_glu, nwb)


def _s5_weights(lam_re, lam_im, log_dt, b_re, b_im, c_re, c_im):
    lr = jnp.minimum(lam_re.astype(F32), C_MIN_NEG_RE)
    li = lam_im.astype(F32)
    dt = jnp.exp(log_dt.astype(F32))[:, None]
    mag = jnp.exp(dt * lr)
    ar, ai = mag * jnp.cos(dt * li), mag * jnp.sin(dt * li)
    den = lr * lr + li * li
    zr = ((ar - 1.0) * lr + ai * li) / den
    zi = (ai * lr - (ar - 1.0) * li) / den
    br, bi = b_re.astype(F32), b_im.astype(F32)
    bbr = zr[..., None] * br - zi[..., None] * bi
    bbi = zr[..., None] * bi + zi[..., None] * br
    cr, ci = c_re.astype(F32), c_im.astype(F32)
    eye_m = jnp.eye(S5_PAIRS, dtype=F32)
    eye_e = jnp.eye(2, dtype=F32)
    split = lambda t: t.reshape((S5_CB, S5_PAIRS, 2) + t.shape[1:])
    bb = jnp.stack([split(bbr), split(bbi)], axis=0)
    bt = jnp.einsum('rjmepc,mn,ef->jnfcmrep', bb, eye_m, eye_e)
    bt = bt.reshape(S5_CB, 256, S5_BLK_STATE)
    cc = jnp.stack([split(cr), -split(ci)], axis=0)
    ct = jnp.einsum('rjmecp,mn,ef->jmrepnfc', cc, eye_m, eye_e)
    ct = ct.reshape(S5_CB, S5_BLK_STATE, 256)
    lanes = lambda t: t.reshape(S5_CB, S5_PAIRS, 2 * C_STATE)
    avec = jnp.concatenate([lanes(ar), lanes(ai)], axis=1)
    return bt.astype(BF16), ct.astype(BF16), avec


def _rotary_tables(seq):
    half = B_HD // 2
    inv_freq = ROPE_THETA ** (-jnp.arange(half, dtype=F32) / half)
    ang = jnp.arange(seq, dtype=jnp.int32).astype(F32)[:, None] * inv_freq[None, :]
    cos, sin = jnp.cos(ang), jnp.sin(ang)
    return jnp.concatenate([cos, cos], axis=-1), jnp.concatenate([-sin, sin], axis=-1)


def kernel(x, mem, norm_w, mem_norm_w, ab_w_in, ab_w_out, hgrn_lb_logits, hgrn_out_norm_w, s5_lambda_re, s5_lambda_im, s5_log_dt, s5_b_re, s5_b_im, s5_c_re, s5_c_im, s5_d, s5_w_glu, xattn_wq, xattn_wkv, xattn_wo, ffn_w_in, ffn_w_out):
    bsz, seq, _ = x.shape
    n = bsz * seq
    depth = norm_w.shape[0]
    nw = lambda layer, i: norm_w[layer, i].astype(F32).reshape(1, D_MODEL)
    lb_table = jnp.cumsum(jax.nn.softmax(hgrn_lb_logits.astype(F32), axis=0), axis=0)
    cos_t, sin_t = _rotary_tables(seq)
    xs = x.reshape(n, D_MODEL)
    for layer in range(depth):
        j = layer // 2
        time_major = layer % 2 == 1
        mixer = None
        if not time_major:
            outs = _proj_ab(xs, nw(layer, 0), ab_w_in[j].astype(BF16), cos_t, sin_t, bsz, seq)
            qa, fa, ia, ga = outs[:4]
            oa = _hgrn(qa, fa, ia, ga, lb_table[layer], hgrn_out_norm_w[j].astype(F32), bsz, seq)
            ob = _dil_attn(outs[4:], bsz, seq)
            mixer = (oa.reshape(n, A_WIDTH), ob.reshape(n, B_WIDTH),
                     ab_w_out[j].astype(BF16), nw(layer, 1))
        else:
            bt, ct, avec = _s5_weights(s5_lambda_re[j], s5_lambda_im[j], s5_log_dt[j],
                                       s5_b_re[j], s5_b_im[j], s5_c_re[j], s5_c_im[j])
            xs = _s5_block(xs, nw(layer, 0), bt, ct, avec,
                           s5_d[j].astype(F32).reshape(1, D_MODEL), s5_w_glu[j].astype(BF16),
                           nw(layer, 1), bsz, seq)
        k, v = _kv_proj(mem, mem_norm_w[layer].astype(F32).reshape(1, D_MODEL),
                        xattn_wkv[layer].astype(BF16))
        next_time_major = layer + 1 < depth and (layer + 1) % 2 == 1
        xs = _layer_tail(xs, mixer, k, v, xattn_wq[layer].astype(BF16),
                         xattn_wo[layer].astype(BF16), nw(layer, 2), nw(layer, 3), nw(layer, 4),
                         ffn_w_in[layer].astype(BF16), ffn_w_out[layer].astype(BF16),
                         nw(layer, 5), bsz, seq, time_major, next_time_major)
    return xs.reshape(bsz, seq, D_MODEL)
```

```python
import functools
import math

import jax
import jax.numpy as jnp
from jax import lax
from jax.experimental import pallas as pl
from jax.experimental.pallas import tpu as pltpu

F32 = jnp.float32
BF16 = jnp.bfloat16

D_MODEL = 1024
NORM_EPS = 1e-6
A_HEADS = 4
A_DK = 128
A_WIDTH = 512
A_CHUNK = 32
B_HEADS = 4
B_HD = 128
B_WIDTH = 512
B_SPAN = 128
B_DILS = (1, 4, 16)
ROPE_THETA = 10000.0
IN_AB_WIDTH = 4 * A_WIDTH + 3 * B_WIDTH
C_GROUP = 16
C_GROUPS = 64
C_STATE = 64
C_MIN_NEG_RE = -1e-4
MEM_LEN = 256
X_HEADS = 4
X_HD = 256
D_FF = 2816

TM = 512
FF_CHUNKS = (1280, 1536)
TAIL_SUBTILES = 2
VMEM_LIMIT = 56 * 1024 * 1024

NT_DIMS = (((1,), (1,)), ((), ()))
TN_DIMS = (((0,), (0,)), ((), ()))


def _cparams(n_axes):
    return pltpu.CompilerParams(
        dimension_semantics=("arbitrary",) * n_axes, vmem_limit_bytes=VMEM_LIMIT)


def _resident(shape):
    zeros = (0,) * len(shape)
    return pl.BlockSpec(shape, lambda *_: zeros, pipeline_mode=pl.Buffered(1))


def _rms(xf, w):
    return xf * lax.rsqrt(jnp.mean(xf * xf, axis=-1, keepdims=True) + NORM_EPS) * w


def _dot(a, b):
    return jnp.dot(a, b, preferred_element_type=F32)


def _proj_ab_kernel(x_ref, nw_ref, w_ref, cos_ref, sin_ref,
                    qa_ref, fa_ref, ia_ref, ga_ref, *b_refs_and_scratch):
    b_refs, stage_s = b_refs_and_scratch[:-1], b_refs_and_scratch[-1]
    tm = x_ref.shape[0]
    h = _rms(x_ref[...], nw_ref[...]).astype(BF16)
    cos = cos_ref[...]
    sin = sin_ref[...]
    for j, o_ref in enumerate((qa_ref, fa_ref, ia_ref, ga_ref)):
        z = _dot(h, w_ref[:, j * 512:(j + 1) * 512])
        for hd in range(4):
            o_ref[hd] = z[:, hd * 128:(hd + 1) * 128].astype(o_ref.dtype)
    for j in range(3):
        z = _dot(h, w_ref[:, (4 + j) * 512:(5 + j) * 512])
        for hd in range(4):
            zh = z[:, hd * 128:(hd + 1) * 128]
            if j < 2:
                zh = zh * cos + pltpu.roll(zh, 64, 1) * sin
            if j == 0:
                zh = zh * (B_HD ** -0.5)
            base = (j * 4 + hd) * tm
            stage_s[base:base + tm, :] = zh
            for d_idx, dil in enumerate(B_DILS):
                o_ref = b_refs[3 * d_idx + j]
                for r in range(dil):
                    rows = pl.ds(base + r, tm // dil, stride=None if dil == 1 else dil)
                    o_ref[hd, :, r * 128:(r + 1) * 128] = stage_s[rows, :].astype(o_ref.dtype)


def _proj_ab(x2, nw, w_in, cos_t, sin_t, bsz, seq):
    n = x2.shape[0]
    tiles = seq // TM
    head_out = lambda dt: jax.ShapeDtypeStruct((4, n, 128), dt)
    head_spec = pl.BlockSpec((4, TM, 128), lambda i: (0, i, 0))
    view_out = lambda d: jax.ShapeDtypeStruct((4, bsz, seq // d, d * 128), BF16)
    view_spec = lambda d: pl.BlockSpec((4, None, TM // d, d * 128),
                                       lambda i: (0, i // tiles, i % tiles, 0))
    return pl.pallas_call(
        _proj_ab_kernel,
        grid=(n // TM,),
        in_specs=[
            pl.BlockSpec((TM, D_MODEL), lambda i: (i, 0)),
            _resident((1, D_MODEL)),
            _resident((D_MODEL, IN_AB_WIDTH)),
            pl.BlockSpec((TM, 128), lambda i: (i % tiles, 0)),
            pl.BlockSpec((TM, 128), lambda i: (i % tiles, 0)),
        ],
        out_specs=[head_spec] * 4 + [view_spec(d) for d in B_DILS for _ in range(3)],
        out_shape=([head_out(BF16), head_out(F32), head_out(BF16), head_out(BF16)]
                   + [view_out(d) for d in B_DILS for _ in range(3)]),
        scratch_shapes=[pltpu.VMEM((12 * TM, 128), F32)],
        compiler_params=_cparams(1),
        name="proj_ab",
    )(x2, nw, w_in, cos_t, sin_t)


HG_BLK = 256
HG_UNROLL = 4


def _hgrn_kernel(q_ref, f_ref, i_ref, g_ref, lb_ref, onw_ref, o_ref,
                 qdec_s, dec_s, oacc_s, upd_s, state_s):
    seq = q_ref.shape[0]
    lb = lb_ref[...]
    ri = lax.broadcasted_iota(jnp.int32, (2 * HG_BLK, HG_BLK), 0)
    ci = lax.broadcasted_iota(jnp.int32, (2 * HG_BLK, HG_BLK), 1)
    rr = jnp.where(ri >= HG_BLK, ri - HG_BLK, ri)
    same = (rr // A_CHUNK) == (ci // A_CHUNK)
    lmat = jnp.where(same & ((ci <= rr) | (ri >= HG_BLK)), 1.0, 0.0).astype(BF16)
    causal = (same & (ci <= rr))[:HG_BLK]

    blocks = [pl.ds(blk * HG_BLK, HG_BLK) for blk in range(seq // HG_BLK)]
    chunks_per_blk = HG_BLK // A_CHUNK
    log_f, k = [], []
    for rows in blocks:
        f = lb + (1.0 - lb) * jax.nn.sigmoid(f_ref[rows, :])
        log_f.append(jnp.log(f))
        k.append(1.0 - f)
    bb = []
    for lf in log_f:
        hi = lf.astype(BF16)
        lo = (lf - hi.astype(F32)).astype(BF16)
        bb.append(_dot(lmat, hi) + _dot(lmat, lo))
    q_dec, k_inv, k_end = [], [], []
    for rows, bbg, kg in zip(blocks, bb, k):
        b, b_last = bbg[:HG_BLK], bbg[HG_BLK:]
        q_dec.append((q_ref[rows, :].astype(F32) * jnp.exp(b)).astype(BF16))
        k_inv.append((kg * jnp.exp(-b)).astype(BF16))
        k_end.append((kg * jnp.exp(b_last - b)).astype(BF16))
        qdec_s[rows, :] = q_dec[-1]
        dec_s[rows, :] = jnp.exp(b_last)
    scores = [lax.dot_general(qg, kg, NT_DIMS, preferred_element_type=F32)
              for qg, kg in zip(q_dec, k_inv)]
    for blk, (rows, keg) in enumerate(zip(blocks, k_end)):
        v_blk = i_ref[rows, :]
        for c in range(chunks_per_blk):
            crow = slice(c * A_CHUNK, (c + 1) * A_CHUNK)
            upd_s[blk * chunks_per_blk + c] = lax.dot_general(
                v_blk[crow], keg[crow], TN_DIMS, preferred_element_type=F32)
    for rows, sg in zip(blocks, scores):
        oacc_s[rows, :] = _dot(jnp.where(causal, sg, 0.0).astype(BF16), i_ref[rows, :])

    def advance(n, state_t):
        state_s[n] = state_t.astype(BF16)
        dec = dec_s[pl.ds(pl.multiple_of(n * A_CHUNK, A_CHUNK), 1), :]
        return dec * state_t + upd_s[n]

    lax.fori_loop(0, seq // A_CHUNK, advance, jnp.zeros((A_DK, A_DK), F32), unroll=HG_UNROLL)

    for blk in range(seq // HG_BLK):
        parts = []
        for c in range(HG_BLK // A_CHUNK):
            n = blk * (HG_BLK // A_CHUNK) + c
            parts.append(lax.dot_general(qdec_s[n * A_CHUNK:(n + 1) * A_CHUNK, :], state_s[n],
                                         NT_DIMS, preferred_element_type=F32))
        rows = pl.ds(blk * HG_BLK, HG_BLK)
        o = oacc_s[rows, :] + jnp.concatenate(parts, axis=0)
        o = o * lax.rsqrt(jnp.mean(o * o, axis=-1, keepdims=True) + NORM_EPS) * onw_ref[...]
        g = g_ref[rows, :].astype(F32)
        o_ref[rows, :] = (o * (g * jax.nn.sigmoid(g))).astype(o_ref.dtype)


def _hgrn(qa, fa, ia, ga, lb, onw, bsz, seq):
    view = lambda t: t.reshape(4, bsz, seq, 128)
    head_spec = pl.BlockSpec((None, None, seq, 128), lambda b, h: (h, b, 0, 0))
    vec_spec = pl.BlockSpec((None, 1, 128), lambda b, h: (h, 0, 0))
    return pl.pallas_call(
        _hgrn_kernel,
        grid=(bsz, A_HEADS),
        in_specs=[head_spec] * 4 + [vec_spec] * 2,
        out_specs=pl.BlockSpec((None, seq, 128), lambda b, h: (b, 0, h)),
        out_shape=jax.ShapeDtypeStruct((bsz, seq, A_WIDTH), BF16),
        scratch_shapes=[pltpu.VMEM((seq, 128), BF16),
                        pltpu.VMEM((seq, 128), F32), pltpu.VMEM((seq, 128), F32),
                        pltpu.VMEM((seq // A_CHUNK, A_DK, A_DK), F32),
                        pltpu.VMEM((seq // A_CHUNK, A_DK, A_DK), BF16)],
        compiler_params=_cparams(2),
        name="hgrn2",
    )(view(qa), view(fa), view(ia), view(ga), lb.reshape(4, 1, 128), onw.reshape(4, 1, 128))


DIL_GROUP = 8


def _dil_kernel(q1_ref, k1_ref, v1_ref, q4_ref, k4_ref, v4_ref, q16_ref, k16_ref, v16_ref,
                o_ref, acc_c, m_c, l_c, acc_n, m_n, l_n):
    seq = q1_ref.shape[0]
    cls4 = seq // 4
    views = {1: (q1_ref, k1_ref, v1_ref), 4: (q4_ref, k4_ref, v4_ref),
             16: (q16_ref, k16_ref, v16_ref)}

    iq = lax.broadcasted_iota(jnp.int32, (B_SPAN, 2 * B_SPAN), 0)
    ik = lax.broadcasted_iota(jnp.int32, (B_SPAN, 2 * B_SPAN), 1)
    mask_first = (ik <= iq)[:, :B_SPAN]
    mask_win = (ik >= iq) & (ik <= iq + B_SPAN)

    def run_branch(dil, state, state_rows, emit):
        q_ref, k_ref, v_ref = views[dil]
        n_blk = seq // (dil * B_SPAN)
        todo = [(r, blk) for r in range(dil) for blk in range(n_blk)]
        for g0 in range(0, len(todo), DIL_GROUP):
            group = todo[g0:g0 + DIL_GROUP]
            scores = []
            for r, blk in group:
                lanes = slice(r * 128, (r + 1) * 128)
                k_rows = slice(max(blk - 1, 0) * B_SPAN, (blk + 1) * B_SPAN)
                s = lax.dot_general(q_ref[blk * B_SPAN:(blk + 1) * B_SPAN, lanes],
                                    k_ref[k_rows, lanes], NT_DIMS, preferred_element_type=F32)
                scores.append(jnp.where(mask_first if blk == 0 else mask_win, s, -jnp.inf))
            probs, stats = [], []
            for (r, blk), s in zip(group, scores):
                m_new = jnp.broadcast_to(jnp.max(s, axis=-1, keepdims=True), (B_SPAN, 128))
                alpha = None
                if state is not None:
                    m_old = state[1][state_rows(r, blk), :]
                    m_new = jnp.maximum(m_old, m_new)
                    alpha = jnp.exp(m_old - m_new)
                p = jnp.exp(s - jnp.concatenate([m_new] * (s.shape[1] // 128), axis=1))
                probs.append(p.astype(BF16))
                stats.append((m_new, alpha, jnp.sum(p, axis=-1, keepdims=True)))
            for (r, blk), p, (m_new, alpha, l_blk) in zip(group, probs, stats):
                lanes = slice(r * 128, (r + 1) * 128)
                k_rows = slice(max(blk - 1, 0) * B_SPAN, (blk + 1) * B_SPAN)
                acc = _dot(p, v_ref[k_rows, lanes])
                l_new = jnp.broadcast_to(l_blk, (B_SPAN, 128))
                if state is not None:
                    rows = state_rows(r, blk)
                    acc = alpha * state[0][rows, :] + acc
                    l_new = alpha * state[2][rows, :] + l_new
                emit(r, blk, acc, m_new, l_new)

    def store_to(refs, rows_of):
        def emit(r, blk, acc, m_new, l_new):
            rows = rows_of(r, blk)
            refs[0][rows, :] = acc
            refs[1][rows, :] = m_new
            refs[2][rows, :] = l_new
        return emit

    class_major = (acc_c, m_c, l_c)
    natural = (acc_n, m_n, l_n)
    rows16 = lambda r, blk: pl.ds((r % 4) * cls4 + r // 4, B_SPAN, stride=4)
    rows4 = lambda r, blk: pl.ds(r * cls4 + blk * B_SPAN, B_SPAN)
    run_branch(16, None, None, store_to(class_major, rows16))
    run_branch(4, class_major, rows4, store_to(class_major, rows4))
    for src, dst in zip(class_major, natural):
        for r in range(4):
            dst[pl.ds(r, cls4, stride=4), :] = src[r * cls4:(r + 1) * cls4, :]
    rows1 = lambda r, blk: pl.ds(blk * B_SPAN, B_SPAN)

    def emit_output(r, blk, acc, m_new, l_new):
        o_ref[rows1(r, blk), :] = (acc / l_new).astype(o_ref.dtype)

    run_branch(1, natural, rows1, emit_output)


def _dil_attn(qkv_views, bsz, seq):
    def spec(d):
        return pl.BlockSpec((None, None, seq // d, d * 128), lambda b, h: (h, b, 0, 0))
    return pl.pallas_call(
        _dil_kernel,
        grid=(bsz, B_HEADS),
        in_specs=[spec(d) for d in B_DILS for _ in range(3)],
        out_specs=pl.BlockSpec((None, seq, 128), lambda b, h: (b, 0, h)),
        out_shape=jax.ShapeDtypeStruct((bsz, seq, B_WIDTH), BF16),
        scratch_shapes=[pltpu.VMEM((seq, 128), F32)] * 6,
        compiler_params=_cparams(2),
        name="dilated_attn",
    )(*qkv_views)


def _kv_kernel(mem_ref, nw_ref, w_ref, k_ref, v_ref):
    h = _rms(mem_ref[...], nw_ref[...]).astype(BF16)
    k_ref[...] = _dot(h, w_ref[:, :D_MODEL]).astype(BF16)
    v_ref[...] = _dot(h, w_ref[:, D_MODEL:]).astype(BF16)


def _kv_proj(mem, nw, wkv):
    bsz = mem.shape[0]
    blk = pl.BlockSpec((None, MEM_LEN, D_MODEL), lambda b: (b, 0, 0))
    kv = jax.ShapeDtypeStruct((bsz, MEM_LEN, D_MODEL), BF16)
    return pl.pallas_call(
        _kv_kernel,
        grid=(bsz,),
        in_specs=[blk, _resident((1, D_MODEL)), _resident((D_MODEL, 2 * D_MODEL))],
        out_specs=[blk, blk],
        out_shape=[kv, kv],
        compiler_params=_cparams(1),
        name="xattn_kv",
    )(mem, nw, wkv)


def _tail_kernel(*refs, with_mixer_proj):
    if with_mixer_proj:
        oa_ref, ob_ref, w_mix_ref, nw_mix_ref = refs[:4]
        refs = refs[4:]
    (x_ref, k_ref, v_ref, wq_ref, wo_ref, nw2_ref, nw3_ref,
     nw4_ref, w_in_ref, w_out_ref, nw5_ref, o_ref) = refs
    sub = x_ref.shape[0] // TAIL_SUBTILES
    groups = [slice(g * sub, (g + 1) * sub) for g in range(TAIL_SUBTILES)]
    each = lambda fn, *lists: [fn(*args) for args in zip(*lists)]

    x = [x_ref[rows, :] for rows in groups]
    if with_mixer_proj:
        y = [_dot(oa_ref[rows, :], w_mix_ref[:A_WIDTH, :])
             + _dot(ob_ref[rows, :], w_mix_ref[A_WIDTH:, :]) for rows in groups]
        x = each(lambda xg, yg: xg + _rms(yg, nw_mix_ref[...]), x, y)

    h = each(lambda xg: _rms(xg, nw2_ref[...]).astype(BF16), x)
    q = each(lambda hg: _dot(hg, wq_ref[...]), h)
    heads = [[] for _ in groups]
    for hd in range(X_HEADS):
        cols = slice(hd * X_HD, (hd + 1) * X_HD)
        s = each(lambda qg: lax.dot_general((qg[:, cols] * (X_HD ** -0.5)).astype(BF16),
                                            k_ref[:, cols], NT_DIMS, preferred_element_type=F32), q)
        p = each(lambda sg: jnp.exp(sg - jnp.max(sg, axis=-1, keepdims=True)), s)
        pv = each(lambda pg: _dot(pg.astype(BF16), v_ref[:, cols]), p)
        for g, (pg, pvg) in enumerate(zip(p, pv)):
            heads[g].append((pvg / jnp.sum(pg, axis=-1, keepdims=True)).astype(BF16))
    y = each(lambda hg: _dot(jnp.concatenate(hg, axis=1), wo_ref[...]), heads)
    x = each(lambda xg, yg: xg + _rms(yg, nw3_ref[...]), x, y)

    h = each(lambda xg: _rms(xg, nw4_ref[...]).astype(BF16), x)
    y = [None] * len(groups)
    lo = 0
    for width in FF_CHUNKS:
        gate, up = slice(lo, lo + width), slice(D_FF + lo, D_FF + lo + width)
        z1 = each(lambda hg: _dot(hg, w_in_ref[:, gate]), h)
        z2 = each(lambda hg: _dot(hg, w_in_ref[:, up]), h)
        a = each(lambda z1g, z2g: (z1g * jax.nn.sigmoid(z1g) * z2g).astype(BF16), z1, z2)
        part = each(lambda ag: _dot(ag, w_out_ref[lo:lo + width, :]), a)
        y = [pg if yg is None else yg + pg for yg, pg in zip(y, part)]
        lo += width
    for rows, xg, yg in zip(groups, x, y):
        o_ref[rows, :] = xg + _rms(yg, nw5_ref[...])


def _row_spec(time_major, seq):
    if time_major:
        return pl.BlockSpec((TM, D_MODEL), lambda b, i: (i, b))
    tiles = seq // TM
    return pl.BlockSpec((TM, D_MODEL), lambda b, i: (b * tiles + i, 0))


def _stream_shape(time_major, bsz, seq):
    shape = (seq, bsz * D_MODEL) if time_major else (bsz * seq, D_MODEL)
    return jax.ShapeDtypeStruct(shape, F32)


def _layer_tail(xs, mixer, k, v, wq, wo, nw2, nw3, nw4, w_in, w_out, nw5, bsz, seq,
                in_time_major, out_time_major):
    tiles = seq // TM
    vec = _resident((1, D_MODEL))
    kvs = pl.BlockSpec((None, MEM_LEN, D_MODEL), lambda b, i: (b, 0, 0))
    in_specs = [_row_spec(in_time_major, seq), kvs, kvs,
                _resident((D_MODEL, D_MODEL)), _resident((D_MODEL, D_MODEL)), vec, vec,
                vec, _resident((D_MODEL, 2 * D_FF)), _resident((D_FF, D_MODEL)), vec]
    operands = [xs, k, v, wq, wo, nw2, nw3, nw4, w_in, w_out, nw5]
    if mixer is not None:
        half = pl.BlockSpec((TM, A_WIDTH), lambda b, i: (b * tiles + i, 0))
        in_specs = [half, half, _resident((D_MODEL, D_MODEL)), vec] + in_specs
        operands = list(mixer) + operands
    return pl.pallas_call(
        functools.partial(_tail_kernel, with_mixer_proj=mixer is not None),
        grid=(bsz, tiles),
        in_specs=in_specs,
        out_specs=_row_spec(out_time_major, seq),
        out_shape=_stream_shape(out_time_major, bsz, seq),
        compiler_params=_cparams(2),
        name="layer_tail",
    )(*operands)


S5_NB = 8
S5_TT = 64
S5_CB = 4
S5_BLK_STATE = 16 * 2 * C_STATE
S5_PAIRS = 8


def _s5_kernel(x_ref, nwa_ref, bt_ref, ct_ref, a_ref, d_ref, wg_ref, nwb_ref, o_ref,
               xs_s, v_s, h_s, y_s, st_s, os_s):
    tt = x_ref.shape[0]
    rows = tt * S5_NB
    lane_slabs = D_MODEL // 128

    @pl.when(pl.program_id(1) == 0)
    def _():
        st_s[...] = jnp.zeros_like(st_s)

    for b in range(S5_NB):
        for k in range(lane_slabs):
            lanes = slice(b * D_MODEL + k * 128, b * D_MODEL + (k + 1) * 128)
            xs_s[pl.ds(k * rows + b, tt, stride=S5_NB), :] = x_ref[:, lanes]

    def load_x():
        return jnp.concatenate([xs_s[k * rows:(k + 1) * rows, :] for k in range(lane_slabs)],
                               axis=1)

    ub = _rms(load_x(), nwa_ref[...]).astype(BF16)

    def project(cb):
        v_s[cb % 2] = _dot(ub[:, cb * 256:(cb + 1) * 256], bt_ref[cb])

    project(0)
    for cb in range(S5_CB):
        vb, hb = v_s.at[cb % 2], h_s.at[cb % 2]
        if cb + 1 < S5_CB:
            project(cb + 1)
        for m in range(S5_PAIRS):
            ar = jnp.broadcast_to(a_ref[cb, m:m + 1, :], (S5_NB, 128))
            ai = jnp.broadcast_to(a_ref[cb, S5_PAIRS + m:S5_PAIRS + m + 1, :], (S5_NB, 128))
            c_re = slice(256 * m, 256 * m + 128)
            c_im = slice(256 * m + 128, 256 * m + 256)
            s_re = slice(cb * S5_BLK_STATE + 256 * m, cb * S5_BLK_STATE + 256 * m + 128)
            s_im = slice(cb * S5_BLK_STATE + 256 * m + 128, cb * S5_BLK_STATE + 256 * m + 256)
            hr, hi = st_s[:, s_re], st_s[:, s_im]
            for i in range(tt // 2):
                hist_r, hist_i = [], []
                for step in range(2):
                    rows_t = slice((2 * i + step) * S5_NB, (2 * i + step + 1) * S5_NB)
                    hr, hi = (ar * hr - ai * hi + vb[rows_t, c_re],
                              ar * hi + ai * hr + vb[rows_t, c_im])
                    hist_r.append(hr)
                    hist_i.append(hi)
                both = slice(2 * i * S5_NB, (2 * i + 2) * S5_NB)
                hb[both, c_re] = jnp.concatenate(hist_r, axis=0).astype(BF16)
                hb[both, c_im] = jnp.concatenate(hist_i, axis=0).astype(BF16)
            st_s[:, s_re] = hr
            st_s[:, s_im] = hi
        y_s[:, cb * 256:(cb + 1) * 256] = _dot(hb[...], ct_ref[cb])

    x = load_x()
    t = y_s[...] + d_ref[...] * _rms(x, nwa_ref[...])
    gl = (0.5 * t * (1.0 + lax.erf(t * math.sqrt(0.5)))).astype(BF16)
    z = _dot(gl, wg_ref[...])
    y = z[:, :D_MODEL] * jax.nn.sigmoid(z[:, D_MODEL:])
    out = x + _rms(y, nwb_ref[...])
    for k in range(lane_slabs):
        os_s[k * rows:(k + 1) * rows, :] = out[:, k * 128:(k + 1) * 128]
    for b in range(S5_NB):
        for k in range(lane_slabs):
            lanes = slice(b * D_MODEL + k * 128, b * D_MODEL + (k + 1) * 128)
            o_ref[:, lanes] = os_s[pl.ds(k * rows + b, tt, stride=S5_NB), :]


def _s5_block(xs, nwa, bt, ct, avec, d, w_glu, nwb, bsz, seq, tt=S5_TT):
    rows = tt * S5_NB
    tile = pl.BlockSpec((tt, S5_NB * D_MODEL), lambda g, i: (i, g))
    return pl.pallas_call(
        _s5_kernel,
        grid=(bsz // S5_NB, seq // tt),
        in_specs=[tile, _resident((1, D_MODEL)), _resident(bt.shape), _resident(ct.shape),
                  _resident(avec.shape), _resident((1, D_MODEL)),
                  _resident((D_MODEL, 2 * D_MODEL)), _resident((1, D_MODEL))],
        out_specs=tile,
        out_shape=_stream_shape(True, bsz, seq),
        scratch_shapes=[pltpu.VMEM((D_MODEL // 128 * rows, 128), F32),
                        pltpu.VMEM((2, rows, S5_BLK_STATE), F32),
                        pltpu.VMEM((2, rows, S5_BLK_STATE), BF16),
                        pltpu.VMEM((rows, D_MODEL), F32),
                        pltpu.VMEM((S5_NB, S5_CB * S5_BLK_STATE), F32),
                        pltpu.VMEM((D_MODEL // 128 * rows, 128), F32)],
        compiler_params=_cparams(2),
        name="s5_block",
    )(xs, nwa, bt, ct, avec, d, w_glu, nwb)


def _s5_weights(lam_re, lam_im, log_dt, b_re, b_im, c_re, c_im):
    lr = jnp.minimum(lam_re.astype(F32), C_MIN_NEG_RE)
    li = lam_im.astype(F32)
    dt = jnp.exp(log_dt.astype(F32))[:, None]
    mag = jnp.exp(dt * lr)
    ar, ai = mag * jnp.cos(dt * li), mag * jnp.sin(dt * li)
    den = lr * lr + li * li
    zr = ((ar - 1.0) * lr + ai * li) / den
    zi = (ai * lr - (ar - 1.0) * li) / den
    br, bi = b_re.astype(F32), b_im.astype(F32)
    bbr = zr[..., None] * br - zi[..., None] * bi
    bbi = zr[..., None] * bi + zi[..., None] * br
    cr, ci = c_re.astype(F32), c_im.astype(F32)
    eye_m = jnp.eye(S5_PAIRS, dtype=F32)
    eye_e = jnp.eye(2, dtype=F32)
    split = lambda t: t.reshape((S5_CB, S5_PAIRS, 2) + t.shape[1:])
    bb = jnp.stack([split(bbr), split(bbi)], axis=0)
    bt = jnp.einsum('rjmepc,mn,ef->jnfcmrep', bb, eye_m, eye_e)
    bt = bt.reshape(S5_CB, 256, S5_BLK_STATE)
    cc = jnp.stack([split(cr), -split(ci)], axis=0)
    ct = jnp.einsum('rjmecp,mn,ef->jmrepnfc', cc, eye_m, eye_e)
    ct = ct.reshape(S5_CB, S5_BLK_STATE, 256)
    lanes = lambda t: t.reshape(S5_CB, S5_PAIRS, 2 * C_STATE)
    avec = jnp.concatenate([lanes(ar), lanes(ai)], axis=1)
    return bt.astype(BF16), ct.astype(BF16), avec


def _rotary_tables(seq):
    half = B_HD // 2
    inv_freq = ROPE_THETA ** (-jnp.arange(half, dtype=F32) / half)
    ang = jnp.arange(seq, dtype=jnp.int32).astype(F32)[:, None] * inv_freq[None, :]
    cos, sin = jnp.cos(ang), jnp.sin(ang)
    return jnp.concatenate([cos, cos], axis=-1), jnp.concatenate([-sin, sin], axis=-1)


def kernel(x, mem, norm_w, mem_norm_w, ab_w_in, ab_w_out, hgrn_lb_logits, hgrn_out_norm_w, s5_lambda_re, s5_lambda_im, s5_log_dt, s5_b_re, s5_b_im, s5_c_re, s5_c_im, s5_d, s5_w_glu, xattn_wq, xattn_wkv, xattn_wo, ffn_w_in, ffn_w_out):
    bsz, seq, _ = x.shape
    n = bsz * seq
    depth = norm_w.shape[0]
    nw = lambda layer, i: norm_w[layer, i].astype(F32).reshape(1, D_MODEL)
    lb_table = jnp.cumsum(jax.nn.softmax(hgrn_lb_logits.astype(F32), axis=0), axis=0)
    cos_t, sin_t = _rotary_tables(seq)
    xs = x.reshape(n, D_MODEL)
    for layer in range(depth):
        j = layer // 2
        time_major = layer % 2 == 1
        mixer = None
        if not time_major:
            outs = _proj_ab(xs, nw(layer, 0), ab_w_in[j].astype(BF16), cos_t, sin_t, bsz, seq)
            qa, fa, ia, ga = outs[:4]
            oa = _hgrn(qa, fa, ia, ga, lb_table[layer], hgrn_out_norm_w[j].astype(F32), bsz, seq)
            ob = _dil_attn(outs[4:], bsz, seq)
            mixer = (oa.reshape(n, A_WIDTH), ob.reshape(n, B_WIDTH),
                     ab_w_out[j].astype(BF16), nw(layer, 1))
        else:
            bt, ct, avec = _s5_weights(s5_lambda_re[j], s5_lambda_im[j], s5_log_dt[j],
                                       s5_b_re[j], s5_b_im[j], s5_c_re[j], s5_c_im[j])
            xs = _s5_block(xs, nw(layer, 0), bt, ct, avec,
                           s5_d[j].astype(F32).reshape(1, D_MODEL), s5_w_glu[j].astype(BF16),
                           nw(layer, 1), bsz, seq)
        k, v = _kv_proj(mem, mem_norm_w[layer].astype(F32).reshape(1, D_MODEL),
                        xattn_wkv[layer].astype(BF16))
        next_time_major = layer + 1 < depth and (layer + 1) % 2 == 1
        xs = _layer_tail(xs, mixer, k, v, xattn_wq[layer].astype(BF16),
                         xattn_wo[layer].astype(BF16), nw(layer, 2), nw(layer, 3), nw(layer, 4),
                         ffn_w_in[layer].astype(BF16), ffn_w_out[layer].astype(BF16),
                         nw(layer, 5), bsz, seq, time_major, next_time_major)
    return xs.reshape(bsz, seq, D_MODEL)
```

```python
import functools
import math

import jax
import jax.numpy as jnp
from jax import lax
from jax.experimental import pallas as pl
from jax.experimental.pallas import tpu as pltpu

F32 = jnp.float32
BF16 = jnp.bfloat16

D_MODEL = 1024
NORM_EPS = 1e-6
A_HEADS = 4
A_DK = 128
A_WIDTH = 512
A_CHUNK = 32
B_HEADS = 4
B_HD = 128
B_WIDTH = 512
B_SPAN = 128
B_DILS = (1, 4, 16)
ROPE_THETA = 10000.0
IN_AB_WIDTH = 4 * A_WIDTH + 3 * B_WIDTH
C_GROUP = 16
C_GROUPS = 64
C_STATE = 64
C_MIN_NEG_RE = -1e-4
MEM_LEN = 256
X_HEADS = 4
X_HD = 256
D_FF = 2816

TM = 512
FF_CHUNKS = (1280, 1536)
TAIL_SUBTILES = 2
VMEM_LIMIT = 56 * 1024 * 1024

NT_DIMS = (((1,), (1,)), ((), ()))
TN_DIMS = (((0,), (0,)), ((), ()))


def _cparams(n_axes):
    return pltpu.CompilerParams(
        dimension_semantics=("arbitrary",) * n_axes, vmem_limit_bytes=VMEM_LIMIT)


def _resident(shape):
    zeros = (0,) * len(shape)
    return pl.BlockSpec(shape, lambda *_: zeros, pipeline_mode=pl.Buffered(1))


def _rms(xf, w):
    return xf * lax.rsqrt(jnp.mean(xf * xf, axis=-1, keepdims=True) + NORM_EPS) * w


def _dot(a, b):
    return jnp.dot(a, b, preferred_element_type=F32)


def _proj_ab_kernel(x_ref, nw_ref, w_ref, cos_ref, sin_ref,
                    qa_ref, fa_ref, ia_ref, ga_ref, *b_refs_and_scratch):
    b_refs, (stage_s, stage4_s) = b_refs_and_scratch[:-2], b_refs_and_scratch[-2:]
    tm = x_ref.shape[0]
    h = _rms(x_ref[...], nw_ref[...]).astype(BF16)
    cos = cos_ref[...]
    sin = sin_ref[...]
    def write_b(j, z):
        for hd in range(4):
            zh = z[:, hd * 128:(hd + 1) * 128]
            if j < 2:
                zh = zh * cos + pltpu.roll(zh, 64, 1) * sin
            if j == 0:
                zh = zh * (B_HD ** -0.5)
            base = (j * 4 + hd) * tm
            o1_ref, o4_ref, o16_ref = (b_refs[3 * d_idx + j] for d_idx in range(3))
            o1_ref[hd] = zh.astype(o1_ref.dtype)
            stage_s[base:base + tm, :] = zh
            for r4 in range(4):
                cls = stage_s[pl.ds(base + r4, tm // 4, stride=4), :]
                o4_ref[hd, :, r4 * 128:(r4 + 1) * 128] = cls.astype(o4_ref.dtype)
                cbase = base + r4 * (tm // 4)
                stage4_s[cbase:cbase + tm // 4, :] = cls
                for w in range(4):
                    r = r4 + 4 * w
                    sub = stage4_s[pl.ds(cbase + w, tm // 16, stride=4), :]
                    o16_ref[hd, :, r * 128:(r + 1) * 128] = sub.astype(o16_ref.dtype)

    def write_a(o_ref, z):
        for hd in range(4):
            o_ref[hd] = z[:, hd * 128:(hd + 1) * 128].astype(o_ref.dtype)

    jobs = [(4 + j, functools.partial(write_b, j)) for j in range(3)]
    jobs += [(j, functools.partial(write_a, o_ref))
             for j, o_ref in enumerate((qa_ref, fa_ref, ia_ref, ga_ref))]
    pending = None
    for c, write in jobs:
        z = _dot(h, w_ref[:, c * 512:(c + 1) * 512])
        if pending is not None:
            pending[0](pending[1])
        pending = (write, z)
    pending[0](pending[1])


def _proj_ab(x2, nw, w_in, cos_t, sin_t, bsz, seq):
    n = x2.shape[0]
    tiles = seq // TM
    head_out = lambda dt: jax.ShapeDtypeStruct((4, n, 128), dt)
    head_spec = pl.BlockSpec((4, TM, 128), lambda i: (0, i, 0))
    view_out = lambda d: jax.ShapeDtypeStruct((4, bsz, seq // d, d * 128), BF16)
    view_spec = lambda d: pl.BlockSpec((4, None, TM // d, d * 128),
                                       lambda i: (0, i // tiles, i % tiles, 0))
    return pl.pallas_call(
        _proj_ab_kernel,
        grid=(n // TM,),
        in_specs=[
            pl.BlockSpec((TM, D_MODEL), lambda i: (i, 0)),
            _resident((1, D_MODEL)),
            _resident((D_MODEL, IN_AB_WIDTH)),
            pl.BlockSpec((TM, 128), lambda i: (i % tiles, 0)),
            pl.BlockSpec((TM, 128), lambda i: (i % tiles, 0)),
        ],
        out_specs=[head_spec] * 4 + [view_spec(d) for d in B_DILS for _ in range(3)],
        out_shape=([head_out(BF16), head_out(F32), head_out(BF16), head_out(BF16)]
                   + [view_out(d) for d in B_DILS for _ in range(3)]),
        scratch_shapes=[pltpu.VMEM((12 * TM, 128), F32)] * 2,
        compiler_params=_cparams(1),
        name="proj_ab",
    )(x2, nw, w_in, cos_t, sin_t)


HG_BLK = 256
HG_UNROLL = 4


def _hgrn_kernel(q_ref, f_ref, i_ref, g_ref, lb_ref, onw_ref, o_ref,
                 qdec_s, dec_s, oacc_s, upd_s, state_s):
    seq = q_ref.shape[0]
    lb = lb_ref[...]
    ri = lax.broadcasted_iota(jnp.int32, (2 * HG_BLK, HG_BLK), 0)
    ci = lax.broadcasted_iota(jnp.int32, (2 * HG_BLK, HG_BLK), 1)
    rr = jnp.where(ri >= HG_BLK, ri - HG_BLK, ri)
    same = (rr // A_CHUNK) == (ci // A_CHUNK)
    lmat = jnp.where(same & ((ci <= rr) | (ri >= HG_BLK)), 1.0, 0.0).astype(BF16)
    causal = (same & (ci <= rr))[:HG_BLK]

    blocks = [pl.ds(blk * HG_BLK, HG_BLK) for blk in range(seq // HG_BLK)]
    chunks_per_blk = HG_BLK // A_CHUNK
    log_f, k = [], []
    for rows in blocks:
        f = lb + (1.0 - lb) * jax.nn.sigmoid(f_ref[rows, :])
        log_f.append(jnp.log(f))
        k.append(1.0 - f)
    bb = []
    for lf in log_f:
        hi = lf.astype(BF16)
        lo = (lf - hi.astype(F32)).astype(BF16)
        bb.append(_dot(lmat, hi) + _dot(lmat, lo))
    q_dec, k_inv, k_end = [], [], []
    for rows, bbg, kg in zip(blocks, bb, k):
        b, b_last = bbg[:HG_BLK], bbg[HG_BLK:]
        q_dec.append((q_ref[rows, :].astype(F32) * jnp.exp(b)).astype(BF16))
        k_inv.append((kg * jnp.exp(-b)).astype(BF16))
        k_end.append((kg * jnp.exp(b_last - b)).astype(BF16))
        qdec_s[rows, :] = q_dec[-1]
        dec_s[rows, :] = jnp.exp(b_last)
    scores = [lax.dot_general(qg, kg, NT_DIMS, preferred_element_type=F32)
              for qg, kg in zip(q_dec, k_inv)]
    for blk, (rows, keg) in enumerate(zip(blocks, k_end)):
        v_blk = i_ref[rows, :]
        for c in range(chunks_per_blk):
            crow = slice(c * A_CHUNK, (c + 1) * A_CHUNK)
            upd_s[blk * chunks_per_blk + c] = lax.dot_general(
                v_blk[crow], keg[crow], TN_DIMS, preferred_element_type=F32)
    for rows, sg in zip(blocks, scores):
        oacc_s[rows, :] = _dot(jnp.where(causal, sg, 0.0).astype(BF16), i_ref[rows, :])

    def advance(n, state_t):
        state_s[n] = state_t.astype(BF16)
        dec = dec_s[pl.ds(pl.multiple_of(n * A_CHUNK, A_CHUNK), 1), :]
        return dec * state_t + upd_s[n]

    lax.fori_loop(0, seq // A_CHUNK, advance, jnp.zeros((A_DK, A_DK), F32), unroll=HG_UNROLL)

    for blk in range(seq // HG_BLK):
        parts = []
        for c in range(HG_BLK // A_CHUNK):
            n = blk * (HG_BLK // A_CHUNK) + c
            parts.append(lax.dot_general(qdec_s[n * A_CHUNK:(n + 1) * A_CHUNK, :], state_s[n],
                                         NT_DIMS, preferred_element_type=F32))
        rows = pl.ds(blk * HG_BLK, HG_BLK)
        o = oacc_s[rows, :] + jnp.concatenate(parts, axis=0)
        o = o * lax.rsqrt(jnp.mean(o * o, axis=-1, keepdims=True) + NORM_EPS) * onw_ref[...]
        g = g_ref[rows, :].astype(F32)
        o_ref[rows, :] = (o * (g * jax.nn.sigmoid(g))).astype(o_ref.dtype)


def _hgrn(qa, fa, ia, ga, lb, onw, bsz, seq):
    view = lambda t: t.reshape(4, bsz, seq, 128)
    head_spec = pl.BlockSpec((None, None, seq, 128), lambda b, h: (h, b, 0, 0))
    vec_spec = pl.BlockSpec((None, 1, 128), lambda b, h: (h, 0, 0))
    return pl.pallas_call(
        _hgrn_kernel,
        grid=(bsz, A_HEADS),
        in_specs=[head_spec] * 4 + [vec_spec] * 2,
        out_specs=pl.BlockSpec((None, seq, 128), lambda b, h: (b, 0, h)),
        out_shape=jax.ShapeDtypeStruct((bsz, seq, A_WIDTH), BF16),
        scratch_shapes=[pltpu.VMEM((seq, 128), BF16),
                        pltpu.VMEM((seq, 128), F32), pltpu.VMEM((seq, 128), F32),
                        pltpu.VMEM((seq // A_CHUNK, A_DK, A_DK), F32),
                        pltpu.VMEM((seq // A_CHUNK, A_DK, A_DK), BF16)],
        compiler_params=_cparams(2),
        name="hgrn2",
    )(view(qa), view(fa), view(ia), view(ga), lb.reshape(4, 1, 128), onw.reshape(4, 1, 128))


DIL_GROUP = 8


def _dil_kernel(q1_ref, k1_ref, v1_ref, q4_ref, k4_ref, v4_ref, q16_ref, k16_ref, v16_ref,
                o_ref, acc_c, m_c, l_c, acc_n, m_n, l_n):
    seq = q1_ref.shape[0]
    cls4 = seq // 4
    views = {1: (q1_ref, k1_ref, v1_ref), 4: (q4_ref, k4_ref, v4_ref),
             16: (q16_ref, k16_ref, v16_ref)}

    iq = lax.broadcasted_iota(jnp.int32, (B_SPAN, 2 * B_SPAN), 0)
    ik = lax.broadcasted_iota(jnp.int32, (B_SPAN, 2 * B_SPAN), 1)
    mask_first = (ik <= iq)[:, :B_SPAN]
    mask_win = (ik >= iq) & (ik <= iq + B_SPAN)

    def run_branch(dil, state, state_rows, emit):
        q_ref, k_ref, v_ref = views[dil]
        n_blk = seq // (dil * B_SPAN)
        todo = [(r, blk) for r in range(dil) for blk in range(n_blk)]
        for g0 in range(0, len(todo), DIL_GROUP):
            group = todo[g0:g0 + DIL_GROUP]
            scores = []
            for r, blk in group:
                lanes = slice(r * 128, (r + 1) * 128)
                k_rows = slice(max(blk - 1, 0) * B_SPAN, (blk + 1) * B_SPAN)
                s = lax.dot_general(q_ref[blk * B_SPAN:(blk + 1) * B_SPAN, lanes],
                                    k_ref[k_rows, lanes], NT_DIMS, preferred_element_type=F32)
                scores.append(jnp.where(mask_first if blk == 0 else mask_win, s, -jnp.inf))
            probs, stats = [], []
            for (r, blk), s in zip(group, scores):
                m_new = jnp.broadcast_to(jnp.max(s, axis=-1, keepdims=True), (B_SPAN, 128))
                alpha = None
                if state is not None:
                    m_old = state[1][state_rows(r, blk), :]
                    m_new = jnp.maximum(m_old, m_new)
                    alpha = jnp.exp(m_old - m_new)
                p = jnp.exp(s - jnp.concatenate([m_new] * (s.shape[1] // 128), axis=1))
                probs.append(p.astype(BF16))
                stats.append((m_new, alpha, jnp.sum(p, axis=-1, keepdims=True)))
            for (r, blk), p, (m_new, alpha, l_blk) in zip(group, probs, stats):
                lanes = slice(r * 128, (r + 1) * 128)
                k_rows = slice(max(blk - 1, 0) * B_SPAN, (blk + 1) * B_SPAN)
                acc = _dot(p, v_ref[k_rows, lanes])
                l_new = jnp.broadcast_to(l_blk, (B_SPAN, 128))
                if state is not None:
                    rows = state_rows(r, blk)
                    acc = alpha * state[0][rows, :] + acc
                    l_new = alpha * state[2][rows, :] + l_new
                emit(r, blk, acc, m_new, l_new)

    def store_to(refs, rows_of):
        def emit(r, blk, acc, m_new, l_new):
            rows = rows_of(r, blk)
            refs[0][rows, :] = acc
            refs[1][rows, :] = m_new
            refs[2][rows, :] = l_new
        return emit

    class_major = (acc_c, m_c, l_c)
    natural = (acc_n, m_n, l_n)
    rows16 = lambda r, blk: pl.ds((r % 4) * cls4 + r // 4, B_SPAN, stride=4)
    rows4 = lambda r, blk: pl.ds(r * cls4 + blk * B_SPAN, B_SPAN)
    run_branch(16, None, None, store_to(class_major, rows16))
    run_branch(4, class_major, rows4, store_to(class_major, rows4))
    for src, dst in zip(class_major, natural):
        for r in range(4):
            dst[pl.ds(r, cls4, stride=4), :] = src[r * cls4:(r + 1) * cls4, :]
    rows1 = lambda r, blk: pl.ds(blk * B_SPAN, B_SPAN)

    def emit_output(r, blk, acc, m_new, l_new):
        o_ref[rows1(r, blk), :] = (acc / l_new).astype(o_ref.dtype)

    run_branch(1, natural, rows1, emit_output)


def _dil_attn(qkv_views, bsz, seq):
    def spec(d):
        return pl.BlockSpec((None, None, seq // d, d * 128), lambda b, h: (h, b, 0, 0))
    return pl.pallas_call(
        _dil_kernel,
        grid=(bsz, B_HEADS),
        in_specs=[spec(d) for d in B_DILS for _ in range(3)],
        out_specs=pl.BlockSpec((None, seq, 128), lambda b, h: (b, 0, h)),
        out_shape=jax.ShapeDtypeStruct((bsz, seq, B_WIDTH), BF16),
        scratch_shapes=[pltpu.VMEM((seq, 128), F32)] * 6,
        compiler_params=_cparams(2),
        name="dilated_attn",
    )(*qkv_views)


def _kv_kernel(mem_ref, nw_ref, w_ref, k_ref, v_ref):
    h = _rms(mem_ref[...], nw_ref[...]).astype(BF16)
    k_ref[...] = _dot(h, w_ref[:, :D_MODEL]).astype(BF16)
    v_ref[...] = _dot(h, w_ref[:, D_MODEL:]).astype(BF16)


def _kv_proj(mem, nw, wkv):
    bsz = mem.shape[0]
    blk = pl.BlockSpec((None, MEM_LEN, D_MODEL), lambda b: (b, 0, 0))
    kv = jax.ShapeDtypeStruct((bsz, MEM_LEN, D_MODEL), BF16)
    return pl.pallas_call(
        _kv_kernel,
        grid=(bsz,),
        in_specs=[blk, _resident((1, D_MODEL)), _resident((D_MODEL, 2 * D_MODEL))],
        out_specs=[blk, blk],
        out_shape=[kv, kv],
        compiler_params=_cparams(1),
        name="xattn_kv",
    )(mem, nw, wkv)


def _tail_kernel(*refs, with_mixer_proj):
    if with_mixer_proj:
        oa_ref, ob_ref, w_mix_ref, nw_mix_ref = refs[:4]
        refs = refs[4:]
    (x_ref, k_ref, v_ref, wq_ref, wo_ref, nw2_ref, nw3_ref,
     nw4_ref, w_in_ref, w_out_ref, nw5_ref, o_ref) = refs
    sub = x_ref.shape[0] // TAIL_SUBTILES
    groups = [slice(g * sub, (g + 1) * sub) for g in range(TAIL_SUBTILES)]
    each = lambda fn, *lists: [fn(*args) for args in zip(*lists)]

    x = [x_ref[rows, :] for rows in groups]
    if with_mixer_proj:
        y = [_dot(oa_ref[rows, :], w_mix_ref[:A_WIDTH, :])
             + _dot(ob_ref[rows, :], w_mix_ref[A_WIDTH:, :]) for rows in groups]
        x = each(lambda xg, yg: xg + _rms(yg, nw_mix_ref[...]), x, y)

    h = each(lambda xg: _rms(xg, nw2_ref[...]).astype(BF16), x)
    q = each(lambda hg: _dot(hg, wq_ref[...]), h)
    heads = [[] for _ in groups]
    for hd in range(X_HEADS):
        cols = slice(hd * X_HD, (hd + 1) * X_HD)
        s = each(lambda qg: lax.dot_general((qg[:, cols] * (X_HD ** -0.5)).astype(BF16),
                                            k_ref[:, cols], NT_DIMS, preferred_element_type=F32), q)
        p = each(lambda sg: jnp.exp(sg - jnp.max(sg, axis=-1, keepdims=True)), s)
        pv = each(lambda pg: _dot(pg.astype(BF16), v_ref[:, cols]), p)
        for g, (pg, pvg) in enumerate(zip(p, pv)):
            heads[g].append((pvg / jnp.sum(pg, axis=-1, keepdims=True)).astype(BF16))
    y = each(lambda hg: _dot(jnp.concatenate(hg, axis=1), wo_ref[...]), heads)
    x = each(lambda xg, yg: xg + _rms(yg, nw3_ref[...]), x, y)

    h = each(lambda xg: _rms(xg, nw4_ref[...]).astype(BF16), x)
    y = [None] * len(groups)
    lo = 0
    for width in FF_CHUNKS:
        gate, up = slice(lo, lo + width), slice(D_FF + lo, D_FF + lo + width)
        z1 = each(lambda hg: _dot(hg, w_in_ref[:, gate]), h)
        z2 = each(lambda hg: _dot(hg, w_in_ref[:, up]), h)
        a = each(lambda z1g, z2g: (z1g * jax.nn.sigmoid(z1g) * z2g).astype(BF16), z1, z2)
        part = each(lambda ag: _dot(ag, w_out_ref[lo:lo + width, :]), a)
        y = [pg if yg is None else yg + pg for yg, pg in zip(y, part)]
        lo += width
    for rows, xg, yg in zip(groups, x, y):
        o_ref[rows, :] = xg + _rms(yg, nw5_ref[...])


def _row_spec(time_major, seq):
    if time_major:
        return pl.BlockSpec((TM, D_MODEL), lambda b, i: (i, b))
    tiles = seq // TM
    return pl.BlockSpec((TM, D_MODEL), lambda b, i: (b * tiles + i, 0))


def _stream_shape(time_major, bsz, seq):
    shape = (seq, bsz * D_MODEL) if time_major else (bsz * seq, D_MODEL)
    return jax.ShapeDtypeStruct(shape, F32)


def _layer_tail(xs, mixer, k, v, wq, wo, nw2, nw3, nw4, w_in, w_out, nw5, bsz, seq,
                in_time_major, out_time_major):
    tiles = seq // TM
    vec = _resident((1, D_MODEL))
    kvs = pl.BlockSpec((None, MEM_LEN, D_MODEL), lambda b, i: (b, 0, 0))
    in_specs = [_row_spec(in_time_major, seq), kvs, kvs,
                _resident((D_MODEL, D_MODEL)), _resident((D_MODEL, D_MODEL)), vec, vec,
                vec, _resident((D_MODEL, 2 * D_FF)), _resident((D_FF, D_MODEL)), vec]
    operands = [xs, k, v, wq, wo, nw2, nw3, nw4, w_in, w_out, nw5]
    if mixer is not None:
        half = pl.BlockSpec((TM, A_WIDTH), lambda b, i: (b * tiles + i, 0))
        in_specs = [half, half, _resident((D_MODEL, D_MODEL)), vec] + in_specs
        operands = list(mixer) + operands
    return pl.pallas_call(
        functools.partial(_tail_kernel, with_mixer_proj=mixer is not None),
        grid=(bsz, tiles),
        in_specs=in_specs,
        out_specs=_row_spec(out_time_major, seq),
        out_shape=_stream_shape(out_time_major, bsz, seq),
        compiler_params=_cparams(2),
        name="layer_tail",
    )(*operands)


S5_NB = 8
S5_TT = 64
S5_CB = 4
S5_BLK_STATE = 16 * 2 * C_STATE
S5_PAIRS = 8


def _s5_kernel(x_ref, nwa_ref, bt_ref, ct_ref, a_ref, d_ref, wg_ref, nwb_ref, o_ref,
               xs_s, v_s, h_s, y_s, st_s, os_s):
    tt = x_ref.shape[0]
    rows = tt * S5_NB
    lane_slabs = D_MODEL // 128

    @pl.when(pl.program_id(1) == 0)
    def _():
        st_s[...] = jnp.zeros_like(st_s)

    for b in range(S5_NB):
        for k in range(lane_slabs):
            lanes = slice(b * D_MODEL + k * 128, b * D_MODEL + (k + 1) * 128)
            xs_s[pl.ds(k * rows + b, tt, stride=S5_NB), :] = x_ref[:, lanes]

    def load_x():
        return jnp.concatenate([xs_s[k * rows:(k + 1) * rows, :] for k in range(lane_slabs)],
                               axis=1)

    ub = _rms(load_x(), nwa_ref[...]).astype(BF16)

    def project(cb):
        v_s[cb % 2] = _dot(ub[:, cb * 256:(cb + 1) * 256], bt_ref[cb])

    project(0)
    for cb in range(S5_CB):
        vb, hb = v_s.at[cb % 2], h_s.at[cb % 2]
        if cb + 1 < S5_CB:
            project(cb + 1)
        for m in range(S5_PAIRS):
            ar = jnp.broadcast_to(a_ref[cb, m:m + 1, :], (S5_NB, 128))
            ai = jnp.broadcast_to(a_ref[cb, S5_PAIRS + m:S5_PAIRS + m + 1, :], (S5_NB, 128))
            c_re = slice(256 * m, 256 * m + 128)
            c_im = slice(256 * m + 128, 256 * m + 256)
            s_re = slice(cb * S5_BLK_STATE + 256 * m, cb * S5_BLK_STATE + 256 * m + 128)
            s_im = slice(cb * S5_BLK_STATE + 256 * m + 128, cb * S5_BLK_STATE + 256 * m + 256)
            hr, hi = st_s[:, s_re], st_s[:, s_im]
            for i in range(tt // 2):
                hist_r, hist_i = [], []
                for step in range(2):
                    rows_t = slice((2 * i + step) * S5_NB, (2 * i + step + 1) * S5_NB)
                    hr, hi = (ar * hr - ai * hi + vb[rows_t, c_re],
                              ar * hi + ai * hr + vb[rows_t, c_im])
                    hist_r.append(hr)
                    hist_i.append(hi)
                both = slice(2 * i * S5_NB, (2 * i + 2) * S5_NB)
                hb[both, c_re] = jnp.concatenate(hist_r, axis=0).astype(BF16)
                hb[both, c_im] = jnp.concatenate(hist_i, axis=0).astype(BF16)
            st_s[:, s_re] = hr
            st_s[:, s_im] = hi
        y_s[:, cb * 256:(cb + 1) * 256] = _dot(hb[...], ct_ref[cb])

    x = load_x()
    t = y_s[...] + d_ref[...] * _rms(x, nwa_ref[...])
    gl = (0.5 * t * (1.0 + lax.erf(t * math.sqrt(0.5)))).astype(BF16)
    z = _dot(gl, wg_ref[...])
    y = z[:, :D_MODEL] * jax.nn.sigmoid(z[:, D_MODEL:])
    out = x + _rms(y, nwb_ref[...])
    for k in range(lane_slabs):
        os_s[k * rows:(k + 1) * rows, :] = out[:, k * 128:(k + 1) * 128]
    for b in range(S5_NB):
        for k in range(lane_slabs):
            lanes = slice(b * D_MODEL + k * 128, b * D_MODEL + (k + 1) * 128)
            o_ref[:, lanes] = os_s[pl.ds(k * rows + b, tt, stride=S5_NB), :]


def _s5_block(xs, nwa, bt, ct, avec, d, w_glu, nwb, bsz, seq, tt=S5_TT):
    rows = tt * S5_NB
    tile = pl.BlockSpec((tt, S5_NB * D_MODEL), lambda g, i: (i, g))
    return pl.pallas_call(
        _s5_kernel,
        grid=(bsz // S5_NB, seq // tt),
        in_specs=[tile, _resident((1, D_MODEL)), _resident(bt.shape), _resident(ct.shape),
                  _resident(avec.shape), _resident((1, D_MODEL)),
                  _resident((D_MODEL, 2 * D_MODEL)), _resident((1, D_MODEL))],
        out_specs=tile,
        out_shape=_stream_shape(True, bsz, seq),
        scratch_shapes=[pltpu.VMEM((D_MODEL // 128 * rows, 128), F32),
                        pltpu.VMEM((2, rows, S5_BLK_STATE), F32),
                        pltpu.VMEM((2, rows, S5_BLK_STATE), BF16),
                        pltpu.VMEM((rows, D_MODEL), F32),
                        pltpu.VMEM((S5_NB, S5_CB * S5_BLK_STATE), F32),
                        pltpu.VMEM((D_MODEL // 128 * rows, 128), F32)],
        compiler_params=_cparams(2),
        name="s5_block",
    )(xs, nwa, bt, ct, avec, d, w_glu, nwb)


def _s5_weights(lam_re, lam_im, log_dt, b_re, b_im, c_re, c_im):
    lr = jnp.minimum(lam_re.astype(F32), C_MIN_NEG_RE)
    li = lam_im.astype(F32)
    dt = jnp.exp(log_dt.astype(F32))[:, None]
    mag = jnp.exp(dt * lr)
    ar, ai = mag * jnp.cos(dt * li), mag * jnp.sin(dt * li)
    den = lr * lr + li * li
    zr = ((ar - 1.0) * lr + ai * li) / den
    zi = (ai * lr - (ar - 1.0) * li) / den
    br, bi = b_re.astype(F32), b_im.astype(F32)
    bbr = zr[..., None] * br - zi[..., None] * bi
    bbi = zr[..., None] * bi + zi[..., None] * br
    cr, ci = c_re.astype(F32), c_im.astype(F32)
    split = lambda t: t.reshape((S5_CB, S5_PAIRS, 2) + t.shape[1:])
    chan_group = jnp.arange(256) // C_GROUP
    col = jnp.arange(S5_BLK_STATE)
    state_group = (col // 256) * 2 + (col // C_STATE) % 2
    same_group = chan_group[:, None] == state_group[None, :]
    bb = jnp.stack([split(bbr), split(bbi)], axis=0)
    strip = bb.transpose(1, 5, 2, 0, 3, 4).reshape(S5_CB, C_GROUP, S5_BLK_STATE)
    bt = jnp.where(same_group[None], jnp.tile(strip, (1, 256 // C_GROUP, 1)), 0.0)
    cc = jnp.stack([split(cr), -split(ci)], axis=0)
    strip = cc.transpose(1, 2, 0, 3, 5, 4).reshape(S5_CB, S5_BLK_STATE, C_GROUP)
    ct = jnp.where(same_group.T[None], jnp.tile(strip, (1, 1, 256 // C_GROUP)), 0.0)
    lanes = lambda t: t.reshape(S5_CB, S5_PAIRS, 2 * C_STATE)
    avec = jnp.concatenate([lanes(ar), lanes(ai)], axis=1)
    return bt.astype(BF16), ct.astype(BF16), avec


def _rotary_tables(seq):
    half = B_HD // 2
    inv_freq = ROPE_THETA ** (-jnp.arange(half, dtype=F32) / half)
    ang = jnp.arange(seq, dtype=jnp.int32).astype(F32)[:, None] * inv_freq[None, :]
    cos, sin = jnp.cos(ang), jnp.sin(ang)
    return jnp.concatenate([cos, cos], axis=-1), jnp.concatenate([-sin, sin], axis=-1)


def kernel(x, mem, norm_w, mem_norm_w, ab_w_in, ab_w_out, hgrn_lb_logits, hgrn_out_norm_w, s5_lambda_re, s5_lambda_im, s5_log_dt, s5_b_re, s5_b_im, s5_c_re, s5_c_im, s5_d, s5_w_glu, xattn_wq, xattn_wkv, xattn_wo, ffn_w_in, ffn_w_out):
    bsz, seq, _ = x.shape
    n = bsz * seq
    depth = norm_w.shape[0]
    nw = lambda layer, i: norm_w[layer, i].astype(F32).reshape(1, D_MODEL)
    lb_table = jnp.cumsum(jax.nn.softmax(hgrn_lb_logits.astype(F32), axis=0), axis=0)
    cos_t, sin_t = _rotary_tables(seq)
    xs = x.reshape(n, D_MODEL)
    for layer in range(depth):
        j = layer // 2
        time_major = layer % 2 == 1
        mixer = None
        if not time_major:
            outs = _proj_ab(xs, nw(layer, 0), ab_w_in[j].astype(BF16), cos_t, sin_t, bsz, seq)
            qa, fa, ia, ga = outs[:4]
            oa = _hgrn(qa, fa, ia, ga, lb_table[layer], hgrn_out_norm_w[j].astype(F32), bsz, seq)
            ob = _dil_attn(outs[4:], bsz, seq)
            mixer = (oa.reshape(n, A_WIDTH), ob.reshape(n, B_WIDTH),
                     ab_w_out[j].astype(BF16), nw(layer, 1))
        else:
            bt, ct, avec = _s5_weights(s5_lambda_re[j], s5_lambda_im[j], s5_log_dt[j],
                                       s5_b_re[j], s5_b_im[j], s5_c_re[j], s5_c_im[j])
            xs = _s5_block(xs, nw(layer, 0), bt, ct, avec,
                           s5_d[j].astype(F32).reshape(1, D_MODEL), s5_w_glu[j].astype(BF16),
                           nw(layer, 1), bsz, seq)
        k, v = _kv_proj(mem, mem_norm_w[layer].astype(F32).reshape(1, D_MODEL),
                        xattn_wkv[layer].astype(BF16))
        next_time_major = layer + 1 < depth and (layer + 1) % 2 == 1
        xs = _layer_tail(xs, mixer, k, v, xattn_wq[layer].astype(BF16),
                         xattn_wo[layer].astype(BF16), nw(layer, 2), nw(layer, 3), nw(layer, 4),
                         ffn_w_in[layer].astype(BF16), ffn_w_out[layer].astype(BF16),
                         nw(layer, 5), bsz, seq, time_major, next_time_major)
    return xs.reshape(bsz, seq, D_MODEL)
```

```python
import functools
import math

import jax
import jax.numpy as jnp
from jax import lax
from jax.experimental import pallas as pl
from jax.experimental.pallas import tpu as pltpu

F32 = jnp.float32
BF16 = jnp.bfloat16

D_MODEL = 1024
NORM_EPS = 1e-6
A_HEADS = 4
A_DK = 128
A_WIDTH = 512
A_CHUNK = 32
B_HEADS = 4
B_HD = 128
B_WIDTH = 512
B_SPAN = 128
B_DILS = (1, 4, 16)
ROPE_THETA = 10000.0
IN_AB_WIDTH = 4 * A_WIDTH + 3 * B_WIDTH
C_GROUP = 16
C_GROUPS = 64
C_STATE = 64
C_MIN_NEG_RE = -1e-4
MEM_LEN = 256
X_HEADS = 4
X_HD = 256
D_FF = 2816

TM = 512
FF_CHUNKS = (1280, 1536)
TAIL_SUBTILES = 2
VMEM_LIMIT = 56 * 1024 * 1024

NT_DIMS = (((1,), (1,)), ((), ()))
TN_DIMS = (((0,), (0,)), ((), ()))


def _cparams(n_axes):
    return pltpu.CompilerParams(
        dimension_semantics=("arbitrary",) * n_axes, vmem_limit_bytes=VMEM_LIMIT)


def _resident(shape):
    zeros = (0,) * len(shape)
    return pl.BlockSpec(shape, lambda *_: zeros, pipeline_mode=pl.Buffered(1))


def _rms(xf, w):
    return xf * lax.rsqrt(jnp.mean(xf * xf, axis=-1, keepdims=True) + NORM_EPS) * w


def _dot(a, b):
    return jnp.dot(a, b, preferred_element_type=F32)


def _proj_ab_kernel(x_ref, nw_ref, w_ref, cos_ref, sin_ref,
                    qa_ref, fa_ref, ia_ref, ga_ref, *b_refs_and_scratch):
    b_refs, (stage_s, stage4_s) = b_refs_and_scratch[:-2], b_refs_and_scratch[-2:]
    tm = x_ref.shape[0]
    h = _rms(x_ref[...], nw_ref[...]).astype(BF16)
    cos = cos_ref[...]
    sin = sin_ref[...]
    def write_b(j, z):
        for hd in range(4):
            zh = z[:, hd * 128:(hd + 1) * 128]
            if j < 2:
                zh = zh * cos + pltpu.roll(zh, 64, 1) * sin
            if j == 0:
                zh = zh * (B_HD ** -0.5)
            base = (j * 4 + hd) * tm
            o1_ref, o4_ref, o16_ref = (b_refs[3 * d_idx + j] for d_idx in range(3))
            o1_ref[hd] = zh.astype(o1_ref.dtype)
            stage_s[base:base + tm, :] = zh
            for r4 in range(4):
                cls = stage_s[pl.ds(base + r4, tm // 4, stride=4), :]
                o4_ref[hd, :, r4 * 128:(r4 + 1) * 128] = cls.astype(o4_ref.dtype)
                cbase = base + r4 * (tm // 4)
                stage4_s[cbase:cbase + tm // 4, :] = cls
                for w in range(4):
                    r = r4 + 4 * w
                    sub = stage4_s[pl.ds(cbase + w, tm // 16, stride=4), :]
                    o16_ref[hd, :, r * 128:(r + 1) * 128] = sub.astype(o16_ref.dtype)

    def write_a(o_ref, z):
        for hd in range(4):
            o_ref[hd] = z[:, hd * 128:(hd + 1) * 128].astype(o_ref.dtype)

    jobs = [(4 + j, functools.partial(write_b, j)) for j in range(3)]
    jobs += [(j, functools.partial(write_a, o_ref))
             for j, o_ref in enumerate((qa_ref, fa_ref, ia_ref, ga_ref))]
    pending = None
    for c, write in jobs:
        z = _dot(h, w_ref[:, c * 512:(c + 1) * 512])
        if pending is not None:
            pending[0](pending[1])
        pending = (write, z)
    pending[0](pending[1])


def _proj_ab(x2, nw, w_in, cos_t, sin_t, bsz, seq):
    n = x2.shape[0]
    tiles = seq // TM
    head_out = lambda dt: jax.ShapeDtypeStruct((4, n, 128), dt)
    head_spec = pl.BlockSpec((4, TM, 128), lambda i: (0, i, 0))
    view_out = lambda d: jax.ShapeDtypeStruct((4, bsz, seq // d, d * 128), BF16)
    view_spec = lambda d: pl.BlockSpec((4, None, TM // d, d * 128),
                                       lambda i: (0, i // tiles, i % tiles, 0))
    return pl.pallas_call(
        _proj_ab_kernel,
        grid=(n // TM,),
        in_specs=[
            pl.BlockSpec((TM, D_MODEL), lambda i: (i, 0)),
            _resident((1, D_MODEL)),
            _resident((D_MODEL, IN_AB_WIDTH)),
            pl.BlockSpec((TM, 128), lambda i: (i % tiles, 0)),
            pl.BlockSpec((TM, 128), lambda i: (i % tiles, 0)),
        ],
        out_specs=[head_spec] * 4 + [view_spec(d) for d in B_DILS for _ in range(3)],
        out_shape=([head_out(BF16), head_out(F32), head_out(BF16), head_out(BF16)]
                   + [view_out(d) for d in B_DILS for _ in range(3)]),
        scratch_shapes=[pltpu.VMEM((12 * TM, 128), F32)] * 2,
        compiler_params=_cparams(1),
        name="proj_ab",
    )(x2, nw, w_in, cos_t, sin_t)


HG_BLK = 256


def _hgrn_stages(q_ref, f_ref, i_ref, g_ref, lb_ref, onw_ref, o_ref,
                 qdec_s, dec_s, oacc_s, upd_s, state_s):
    seq = q_ref.shape[0]
    lb = lb_ref[...]
    ri = lax.broadcasted_iota(jnp.int32, (2 * HG_BLK, HG_BLK), 0)
    ci = lax.broadcasted_iota(jnp.int32, (2 * HG_BLK, HG_BLK), 1)
    rr = jnp.where(ri >= HG_BLK, ri - HG_BLK, ri)
    same = (rr // A_CHUNK) == (ci // A_CHUNK)
    lmat = jnp.where(same & ((ci <= rr) | (ri >= HG_BLK)), 1.0, 0.0).astype(BF16)
    causal = (same & (ci <= rr))[:HG_BLK]

    blocks = [pl.ds(blk * HG_BLK, HG_BLK) for blk in range(seq // HG_BLK)]
    chunks_per_blk = HG_BLK // A_CHUNK
    log_f, k = [], []
    for rows in blocks:
        f = lb + (1.0 - lb) * jax.nn.sigmoid(f_ref[rows, :])
        log_f.append(jnp.log(f))
        k.append(1.0 - f)
    yield
    bb = []
    for lf in log_f:
        hi = lf.astype(BF16)
        lo = (lf - hi.astype(F32)).astype(BF16)
        bb.append(_dot(lmat, hi) + _dot(lmat, lo))
    yield
    q_dec, k_inv, k_end = [], [], []
    for rows, bbg, kg in zip(blocks, bb, k):
        b, b_last = bbg[:HG_BLK], bbg[HG_BLK:]
        q_dec.append((q_ref[rows, :].astype(F32) * jnp.exp(b)).astype(BF16))
        k_inv.append((kg * jnp.exp(-b)).astype(BF16))
        k_end.append((kg * jnp.exp(b_last - b)).astype(BF16))
        qdec_s[rows, :] = q_dec[-1]
        dec_s[rows, :] = jnp.exp(b_last)
    yield
    scores = [lax.dot_general(qg, kg, NT_DIMS, preferred_element_type=F32)
              for qg, kg in zip(q_dec, k_inv)]
    yield
    for blk, (rows, keg) in enumerate(zip(blocks, k_end)):
        v_blk = i_ref[rows, :]
        for c in range(chunks_per_blk):
            crow = slice(c * A_CHUNK, (c + 1) * A_CHUNK)
            upd_s[blk * chunks_per_blk + c] = lax.dot_general(
                v_blk[crow], keg[crow], TN_DIMS, preferred_element_type=F32)
        if blk % 2 == 1:
            yield
    for rows, sg in zip(blocks, scores):
        oacc_s[rows, :] = _dot(jnp.where(causal, sg, 0.0).astype(BF16), i_ref[rows, :])
    yield

    state_t = jnp.zeros((A_DK, A_DK), F32)
    for blk, rows in enumerate(blocks):
        parts = []
        for c in range(chunks_per_blk):
            n = blk * chunks_per_blk + c
            state_s[n] = state_t.astype(BF16)
            state_t = dec_s[n * A_CHUNK:n * A_CHUNK + 1, :] * state_t + upd_s[n]
        yield
        for c in range(chunks_per_blk):
            n = blk * chunks_per_blk + c
            parts.append(lax.dot_general(qdec_s[n * A_CHUNK:(n + 1) * A_CHUNK, :], state_s[n],
                                         NT_DIMS, preferred_element_type=F32))
        o = oacc_s[rows, :] + jnp.concatenate(parts, axis=0)
        o = o * lax.rsqrt(jnp.mean(o * o, axis=-1, keepdims=True) + NORM_EPS) * onw_ref[...]
        g = g_ref[rows, :].astype(F32)
        o_ref[rows, :] = (o * (g * jax.nn.sigmoid(g))).astype(o_ref.dtype)
        yield


DIL_GROUP = 8


def _dil_stages(q1_ref, k1_ref, v1_ref, q4_ref, k4_ref, v4_ref, q16_ref, k16_ref, v16_ref,
                o_ref, acc_c, m_c, l_c, acc_n, m_n, l_n):
    seq = q1_ref.shape[0]
    cls4 = seq // 4
    views = {1: (q1_ref, k1_ref, v1_ref), 4: (q4_ref, k4_ref, v4_ref),
             16: (q16_ref, k16_ref, v16_ref)}

    iq = lax.broadcasted_iota(jnp.int32, (B_SPAN, 2 * B_SPAN), 0)
    ik = lax.broadcasted_iota(jnp.int32, (B_SPAN, 2 * B_SPAN), 1)
    mask_first = (ik <= iq)[:, :B_SPAN]
    mask_win = (ik >= iq) & (ik <= iq + B_SPAN)

    def run_branch(dil, state, state_rows, emit):
        q_ref, k_ref, v_ref = views[dil]
        n_blk = seq // (dil * B_SPAN)
        todo = [(r, blk) for r in range(dil) for blk in range(n_blk)]
        for g0 in range(0, len(todo), DIL_GROUP):
            group = todo[g0:g0 + DIL_GROUP]
            scores = []
            for r, blk in group:
                lanes = slice(r * 128, (r + 1) * 128)
                k_rows = slice(max(blk - 1, 0) * B_SPAN, (blk + 1) * B_SPAN)
                s = lax.dot_general(q_ref[blk * B_SPAN:(blk + 1) * B_SPAN, lanes],
                                    k_ref[k_rows, lanes], NT_DIMS, preferred_element_type=F32)
                scores.append(jnp.where(mask_first if blk == 0 else mask_win, s, -jnp.inf))
            yield
            probs, stats = [], []
            for (r, blk), s in zip(group, scores):
                m_new = jnp.broadcast_to(jnp.max(s, axis=-1, keepdims=True), (B_SPAN, 128))
                alpha = None
                if state is not None:
                    m_old = state[1][state_rows(r, blk), :]
                    m_new = jnp.maximum(m_old, m_new)
                    alpha = jnp.exp(m_old - m_new)
                p = jnp.exp(s - jnp.concatenate([m_new] * (s.shape[1] // 128), axis=1))
                probs.append(p.astype(BF16))
                stats.append((m_new, alpha, jnp.sum(p, axis=-1, keepdims=True)))
            yield
            for (r, blk), p, (m_new, alpha, l_blk) in zip(group, probs, stats):
                lanes = slice(r * 128, (r + 1) * 128)
                k_rows = slice(max(blk - 1, 0) * B_SPAN, (blk + 1) * B_SPAN)
                acc = _dot(p, v_ref[k_rows, lanes])
                l_new = jnp.broadcast_to(l_blk, (B_SPAN, 128))
                if state is not None:
                    rows = state_rows(r, blk)
                    acc = alpha * state[0][rows, :] + acc
                    l_new = alpha * state[2][rows, :] + l_new
                emit(r, blk, acc, m_new, l_new)
            yield

    def store_to(refs, rows_of):
        def emit(r, blk, acc, m_new, l_new):
            rows = rows_of(r, blk)
            refs[0][rows, :] = acc
            refs[1][rows, :] = m_new
            refs[2][rows, :] = l_new
        return emit

    class_major = (acc_c, m_c, l_c)
    natural = (acc_n, m_n, l_n)
    rows16 = lambda r, blk: pl.ds((r % 4) * cls4 + r // 4, B_SPAN, stride=4)
    rows4 = lambda r, blk: pl.ds(r * cls4 + blk * B_SPAN, B_SPAN)
    yield from run_branch(16, None, None, store_to(class_major, rows16))
    yield from run_branch(4, class_major, rows4, store_to(class_major, rows4))
    for src, dst in zip(class_major, natural):
        for r in range(4):
            dst[pl.ds(r, cls4, stride=4), :] = src[r * cls4:(r + 1) * cls4, :]
    yield
    rows1 = lambda r, blk: pl.ds(blk * B_SPAN, B_SPAN)

    def emit_output(r, blk, acc, m_new, l_new):
        o_ref[rows1(r, blk), :] = (acc / l_new).astype(o_ref.dtype)

    yield from run_branch(1, natural, rows1, emit_output)


N_HGRN_IN, N_DIL_IN, N_HGRN_SCRATCH = 6, 9, 5


def _mixers_kernel(*refs):
    hgrn_in, refs = refs[:N_HGRN_IN], refs[N_HGRN_IN:]
    dil_in, refs = refs[:N_DIL_IN], refs[N_DIL_IN:]
    oa_ref, ob_ref = refs[:2]
    hgrn_scratch, dil_scratch = refs[2:2 + N_HGRN_SCRATCH], refs[2 + N_HGRN_SCRATCH:]
    pending = [_hgrn_stages(*hgrn_in, oa_ref, *hgrn_scratch),
               _dil_stages(*dil_in, ob_ref, *dil_scratch)]
    while pending:
        for stages in list(pending):
            if next(stages, StopIteration) is StopIteration:
                pending.remove(stages)


def _mixers(qa, fa, ia, ga, lb, onw, qkv_views, bsz, seq):
    view = lambda t: t.reshape(4, bsz, seq, 128)
    head_spec = pl.BlockSpec((None, None, seq, 128), lambda b, h: (h, b, 0, 0))
    vec_spec = pl.BlockSpec((None, 1, 128), lambda b, h: (h, 0, 0))
    dil_spec = lambda d: pl.BlockSpec((None, None, seq // d, d * 128), lambda b, h: (h, b, 0, 0))
    out_spec = pl.BlockSpec((None, seq, 128), lambda b, h: (b, 0, h))
    return pl.pallas_call(
        _mixers_kernel,
        grid=(bsz, A_HEADS),
        in_specs=([head_spec] * 4 + [vec_spec] * 2
                  + [dil_spec(d) for d in B_DILS for _ in range(3)]),
        out_specs=[out_spec, out_spec],
        out_shape=[jax.ShapeDtypeStruct((bsz, seq, A_WIDTH), BF16),
                   jax.ShapeDtypeStruct((bsz, seq, B_WIDTH), BF16)],
        scratch_shapes=([pltpu.VMEM((seq, 128), BF16),
                         pltpu.VMEM((seq, 128), F32), pltpu.VMEM((seq, 128), F32),
                         pltpu.VMEM((seq // A_CHUNK, A_DK, A_DK), F32),
                         pltpu.VMEM((seq // A_CHUNK, A_DK, A_DK), BF16)]
                        + [pltpu.VMEM((seq, 128), F32)] * 6),
        compiler_params=_cparams(2),
        name="mixers_ab",
    )(view(qa), view(fa), view(ia), view(ga), lb.reshape(4, 1, 128), onw.reshape(4, 1, 128),
      *qkv_views)


KV_BATCH = 4


def _kv_kernel(mem_ref, nw_ref, w_ref, k_ref, v_ref):
    rows = mem_ref.shape[0] * mem_ref.shape[1]
    h = _rms(mem_ref[...].reshape(rows, D_MODEL), nw_ref[...]).astype(BF16)
    k_ref[...] = _dot(h, w_ref[:, :D_MODEL]).astype(BF16).reshape(k_ref.shape)
    v_ref[...] = _dot(h, w_ref[:, D_MODEL:]).astype(BF16).reshape(v_ref.shape)


def _kv_proj(mem, nw, wkv):
    bsz = mem.shape[0]
    blk = pl.BlockSpec((KV_BATCH, MEM_LEN, D_MODEL), lambda b: (b, 0, 0))
    kv = jax.ShapeDtypeStruct((bsz, MEM_LEN, D_MODEL), BF16)
    return pl.pallas_call(
        _kv_kernel,
        grid=(bsz // KV_BATCH,),
        in_specs=[blk, _resident((1, D_MODEL)), _resident((D_MODEL, 2 * D_MODEL))],
        out_specs=[blk, blk],
        out_shape=[kv, kv],
        compiler_params=_cparams(1),
        name="xattn_kv",
    )(mem, nw, wkv)


def _tail_kernel(*refs, with_mixer_proj):
    if with_mixer_proj:
        oa_ref, ob_ref, w_mix_ref, nw_mix_ref = refs[:4]
        refs = refs[4:]
    (x_ref, k_ref, v_ref, wq_ref, wo_ref, nw2_ref, nw3_ref,
     nw4_ref, w_in_ref, w_out_ref, nw5_ref, o_ref) = refs
    sub = x_ref.shape[0] // TAIL_SUBTILES
    groups = [slice(g * sub, (g + 1) * sub) for g in range(TAIL_SUBTILES)]
    each = lambda fn, *lists: [fn(*args) for args in zip(*lists)]

    x = [x_ref[rows, :] for rows in groups]
    if with_mixer_proj:
        y = [_dot(oa_ref[rows, :], w_mix_ref[:A_WIDTH, :])
             + _dot(ob_ref[rows, :], w_mix_ref[A_WIDTH:, :]) for rows in groups]
        x = each(lambda xg, yg: xg + _rms(yg, nw_mix_ref[...]), x, y)

    h = each(lambda xg: _rms(xg, nw2_ref[...]).astype(BF16), x)
    q = each(lambda hg: _dot(hg, wq_ref[...]), h)
    heads = [[] for _ in groups]
    for hd in range(X_HEADS):
        cols = slice(hd * X_HD, (hd + 1) * X_HD)
        s = each(lambda qg: lax.dot_general((qg[:, cols] * (X_HD ** -0.5)).astype(BF16),
                                            k_ref[:, cols], NT_DIMS, preferred_element_type=F32), q)
        p = each(lambda sg: jnp.exp(sg - jnp.max(sg, axis=-1, keepdims=True)), s)
        pv = each(lambda pg: _dot(pg.astype(BF16), v_ref[:, cols]), p)
        for g, (pg, pvg) in enumerate(zip(p, pv)):
            heads[g].append((pvg / jnp.sum(pg, axis=-1, keepdims=True)).astype(BF16))
    y = each(lambda hg: _dot(jnp.concatenate(hg, axis=1), wo_ref[...]), heads)
    x = each(lambda xg, yg: xg + _rms(yg, nw3_ref[...]), x, y)

    h = each(lambda xg: _rms(xg, nw4_ref[...]).astype(BF16), x)
    y = [None] * len(groups)
    lo = 0
    for width in FF_CHUNKS:
        gate, up = slice(lo, lo + width), slice(D_FF + lo, D_FF + lo + width)
        z1 = each(lambda hg: _dot(hg, w_in_ref[:, gate]), h)
        z2 = each(lambda hg: _dot(hg, w_in_ref[:, up]), h)
        a = each(lambda z1g, z2g: (z1g * jax.nn.sigmoid(z1g) * z2g).astype(BF16), z1, z2)
        part = each(lambda ag: _dot(ag, w_out_ref[lo:lo + width, :]), a)
        y = [pg if yg is None else yg + pg for yg, pg in zip(y, part)]
        lo += width
    for rows, xg, yg in zip(groups, x, y):
        o_ref[rows, :] = xg + _rms(yg, nw5_ref[...])


def _row_spec(time_major, seq):
    if time_major:
        return pl.BlockSpec((TM, D_MODEL), lambda b, i: (i, b))
    tiles = seq // TM
    return pl.BlockSpec((TM, D_MODEL), lambda b, i: (b * tiles + i, 0))


def _stream_shape(time_major, bsz, seq):
    shape = (seq, bsz * D_MODEL) if time_major else (bsz * seq, D_MODEL)
    return jax.ShapeDtypeStruct(shape, F32)


def _layer_tail(xs, mixer, k, v, wq, wo, nw2, nw3, nw4, w_in, w_out, nw5, bsz, seq,
                in_time_major, out_time_major):
    tiles = seq // TM
    vec = _resident((1, D_MODEL))
    kvs = pl.BlockSpec((None, MEM_LEN, D_MODEL), lambda b, i: (b, 0, 0))
    in_specs = [_row_spec(in_time_major, seq), kvs, kvs,
                _resident((D_MODEL, D_MODEL)), _resident((D_MODEL, D_MODEL)), vec, vec,
                vec, _resident((D_MODEL, 2 * D_FF)), _resident((D_FF, D_MODEL)), vec]
    operands = [xs, k, v, wq, wo, nw2, nw3, nw4, w_in, w_out, nw5]
    if mixer is not None:
        half = pl.BlockSpec((TM, A_WIDTH), lambda b, i: (b * tiles + i, 0))
        in_specs = [half, half, _resident((D_MODEL, D_MODEL)), vec] + in_specs
        operands = list(mixer) + operands
    return pl.pallas_call(
        functools.partial(_tail_kernel, with_mixer_proj=mixer is not None),
        grid=(bsz, tiles),
        in_specs=in_specs,
        out_specs=_row_spec(out_time_major, seq),
        out_shape=_stream_shape(out_time_major, bsz, seq),
        compiler_params=_cparams(2),
        name="layer_tail",
    )(*operands)


S5_NB = 8
S5_TT = 64
S5_CB = 4
S5_BLK_STATE = 16 * 2 * C_STATE
S5_PAIRS = 8


def _s5_kernel(x_ref, nwa_ref, bt_ref, ct_ref, a_ref, d_ref, wg_ref, nwb_ref, o_ref,
               xs_s, v_s, h_s, y_s, st_s, os_s):
    tt = x_ref.shape[0]
    rows = tt * S5_NB
    lane_slabs = D_MODEL // 128

    @pl.when(pl.program_id(1) == 0)
    def _():
        st_s[...] = jnp.zeros_like(st_s)

    for b in range(S5_NB):
        for k in range(lane_slabs):
            lanes = slice(b * D_MODEL + k * 128, b * D_MODEL + (k + 1) * 128)
            xs_s[pl.ds(k * rows + b, tt, stride=S5_NB), :] = x_ref[:, lanes]

    def load_x():
        return jnp.concatenate([xs_s[k * rows:(k + 1) * rows, :] for k in range(lane_slabs)],
                               axis=1)

    ub = _rms(load_x(), nwa_ref[...]).astype(BF16)

    def project(cb):
        v_s[cb % 2] = _dot(ub[:, cb * 256:(cb + 1) * 256], bt_ref[cb])

    project(0)
    for cb in range(S5_CB):
        vb, hb = v_s.at[cb % 2], h_s.at[cb % 2]
        if cb + 1 < S5_CB:
            project(cb + 1)
        for m in range(S5_PAIRS):
            ar = jnp.broadcast_to(a_ref[cb, m:m + 1, :], (S5_NB, 128))
            ai = jnp.broadcast_to(a_ref[cb, S5_PAIRS + m:S5_PAIRS + m + 1, :], (S5_NB, 128))
            c_re = slice(256 * m, 256 * m + 128)
            c_im = slice(256 * m + 128, 256 * m + 256)
            s_re = slice(cb * S5_BLK_STATE + 256 * m, cb * S5_BLK_STATE + 256 * m + 128)
            s_im = slice(cb * S5_BLK_STATE + 256 * m + 128, cb * S5_BLK_STATE + 256 * m + 256)
            hr, hi = st_s[:, s_re], st_s[:, s_im]
            for i in range(tt // 2):
                hist_r, hist_i = [], []
                for step in range(2):
                    rows_t = slice((2 * i + step) * S5_NB, (2 * i + step + 1) * S5_NB)
                    hr, hi = (ar * hr - ai * hi + vb[rows_t, c_re],
                              ar * hi + ai * hr + vb[rows_t, c_im])
                    hist_r.append(hr)
                    hist_i.append(hi)
                both = slice(2 * i * S5_NB, (2 * i + 2) * S5_NB)
                hb[both, c_re] = jnp.concatenate(hist_r, axis=0).astype(BF16)
                hb[both, c_im] = jnp.concatenate(hist_i, axis=0).astype(BF16)
            st_s[:, s_re] = hr
            st_s[:, s_im] = hi
        y_s[:, cb * 256:(cb + 1) * 256] = _dot(hb[...], ct_ref[cb])

    x = load_x()
    t = y_s[...] + d_ref[...] * _rms(x, nwa_ref[...])
    gl = (0.5 * t * (1.0 + lax.erf(t * math.sqrt(0.5)))).astype(BF16)
    z = _dot(gl, wg_ref[...])
    y = z[:, :D_MODEL] * jax.nn.sigmoid(z[:, D_MODEL:])
    out = x + _rms(y, nwb_ref[...])
    for k in range(lane_slabs):
        os_s[k * rows:(k + 1) * rows, :] = out[:, k * 128:(k + 1) * 128]
    for b in range(S5_NB):
        for k in range(lane_slabs):
            lanes = slice(b * D_MODEL + k * 128, b * D_MODEL + (k + 1) * 128)
            o_ref[:, lanes] = os_s[pl.ds(k * rows + b, tt, stride=S5_NB), :]


def _s5_block(xs, nwa, bt, ct, avec, d, w_glu, nwb, bsz, seq, tt=S5_TT):
    rows = tt * S5_NB
    tile = pl.BlockSpec((tt, S5_NB * D_MODEL), lambda g, i: (i, g))
    return pl.pallas_call(
        _s5_kernel,
        grid=(bsz // S5_NB, seq // tt),
        in_specs=[tile, _resident((1, D_MODEL)), _resident(bt.shape), _resident(ct.shape),
                  _resident(avec.shape), _resident((1, D_MODEL)),
                  _resident((D_MODEL, 2 * D_MODEL)), _resident((1, D_MODEL))],
        out_specs=tile,
        out_shape=_stream_shape(True, bsz, seq),
        scratch_shapes=[pltpu.VMEM((D_MODEL // 128 * rows, 128), F32),
                        pltpu.VMEM((2, rows, S5_BLK_STATE), F32),
                        pltpu.VMEM((2, rows, S5_BLK_STATE), BF16),
                        pltpu.VMEM((rows, D_MODEL), F32),
                        pltpu.VMEM((S5_NB, S5_CB * S5_BLK_STATE), F32),
                        pltpu.VMEM((D_MODEL // 128 * rows, 128), F32)],
        compiler_params=_cparams(2),
        name="s5_block",
    )(xs, nwa, bt, ct, avec, d, w_glu, nwb)


def _s5_weights(lam_re, lam_im, log_dt, b_re, b_im, c_re, c_im):
    lr = jnp.minimum(lam_re.astype(F32), C_MIN_NEG_RE)
    li = lam_im.astype(F32)
    dt = jnp.exp(log_dt.astype(F32))[:, None]
    mag = jnp.exp(dt * lr)
    ar, ai = mag * jnp.cos(dt * li), mag * jnp.sin(dt * li)
    den = lr * lr + li * li
    zr = ((ar - 1.0) * lr + ai * li) / den
    zi = (ai * lr - (ar - 1.0) * li) / den
    br, bi = b_re.astype(F32), b_im.astype(F32)
    bbr = zr[..., None] * br - zi[..., None] * bi
    bbi = zr[..., None] * bi + zi[..., None] * br
    cr, ci = c_re.astype(F32), c_im.astype(F32)
    split = lambda t: t.reshape((S5_CB, S5_PAIRS, 2) + t.shape[1:])
    chan_group = jnp.arange(256) // C_GROUP
    col = jnp.arange(S5_BLK_STATE)
    state_group = (col // 256) * 2 + (col // C_STATE) % 2
    same_group = chan_group[:, None] == state_group[None, :]
    bb = jnp.stack([split(bbr), split(bbi)], axis=0)
    strip = bb.transpose(1, 5, 2, 0, 3, 4).reshape(S5_CB, C_GROUP, S5_BLK_STATE)
    bt = jnp.where(same_group[None], jnp.tile(strip, (1, 256 // C_GROUP, 1)), 0.0)
    cc = jnp.stack([split(cr), -split(ci)], axis=0)
    strip = cc.transpose(1, 2, 0, 3, 5, 4).reshape(S5_CB, S5_BLK_STATE, C_GROUP)
    ct = jnp.where(same_group.T[None], jnp.tile(strip, (1, 1, 256 // C_GROUP)), 0.0)
    lanes = lambda t: t.reshape(S5_CB, S5_PAIRS, 2 * C_STATE)
    avec = jnp.concatenate([lanes(ar), lanes(ai)], axis=1)
    return bt.astype(BF16), ct.astype(BF16), avec


def _rotary_tables(seq):
    half = B_HD // 2
    inv_freq = ROPE_THETA ** (-jnp.arange(half, dtype=F32) / half)
    ang = jnp.arange(seq, dtype=jnp.int32).astype(F32)[:, None] * inv_freq[None, :]
    cos, sin = jnp.cos(ang), jnp.sin(ang)
    return jnp.concatenate([cos, cos], axis=-1), jnp.concatenate([-sin, sin], axis=-1)


def kernel(x, mem, norm_w, mem_norm_w, ab_w_in, ab_w_out, hgrn_lb_logits, hgrn_out_norm_w, s5_lambda_re, s5_lambda_im, s5_log_dt, s5_b_re, s5_b_im, s5_c_re, s5_c_im, s5_d, s5_w_glu, xattn_wq, xattn_wkv, xattn_wo, ffn_w_in, ffn_w_out):
    bsz, seq, _ = x.shape
    n = bsz * seq
    depth = norm_w.shape[0]
    nw = lambda layer, i: norm_w[layer, i].astype(F32).reshape(1, D_MODEL)
    lb_table = jnp.cumsum(jax.nn.softmax(hgrn_lb_logits.astype(F32), axis=0), axis=0)
    cos_t, sin_t = _rotary_tables(seq)
    xs = x.reshape(n, D_MODEL)
    for layer in range(depth):
        j = layer // 2
        time_major = layer % 2 == 1
        mixer = None
        if not time_major:
            outs = _proj_ab(xs, nw(layer, 0), ab_w_in[j].astype(BF16), cos_t, sin_t, bsz, seq)
            qa, fa, ia, ga = outs[:4]
            oa, ob = _mixers(qa, fa, ia, ga, lb_table[layer], hgrn_out_norm_w[j].astype(F32),
                             outs[4:], bsz, seq)
            mixer = (oa.reshape(n, A_WIDTH), ob.reshape(n, B_WIDTH),
                     ab_w_out[j].astype(BF16), nw(layer, 1))
        else:
            bt, ct, avec = _s5_weights(s5_lambda_re[j], s5_lambda_im[j], s5_log_dt[j],
                                       s5_b_re[j], s5_b_im[j], s5_c_re[j], s5_c_im[j])
            xs = _s5_block(xs, nw(layer, 0), bt, ct, avec,
                           s5_d[j].astype(F32).reshape(1, D_MODEL), s5_w_glu[j].astype(BF16),
                           nw(layer, 1), bsz, seq)
        k, v = _kv_proj(mem, mem_norm_w[layer].astype(F32).reshape(1, D_MODEL),
                        xattn_wkv[layer].astype(BF16))
        next_time_major = layer + 1 < depth and (layer + 1) % 2 == 1
        xs = _layer_tail(xs, mixer, k, v, xattn_wq[layer].astype(BF16),
                         xattn_wo[layer].astype(BF16), nw(layer, 2), nw(layer, 3), nw(layer, 4),
                         ffn_w_in[layer].astype(BF16), ffn_w_out[layer].astype(BF16),
                         nw(layer, 5), bsz, seq, time_major, next_time_major)
    return xs.reshape(bsz, seq, D_MODEL)
```

```python
import functools
import math

import jax
import jax.numpy as jnp
from jax import lax
from jax.experimental import pallas as pl
from jax.experimental.pallas import tpu as pltpu

F32 = jnp.float32
BF16 = jnp.bfloat16

D_MODEL = 1024
NORM_EPS = 1e-6
A_HEADS = 4
A_DK = 128
A_WIDTH = 512
A_CHUNK = 32
B_HEADS = 4
B_HD = 128
B_WIDTH = 512
B_SPAN = 128
B_DILS = (1, 4, 16)
ROPE_THETA = 10000.0
IN_AB_WIDTH = 4 * A_WIDTH + 3 * B_WIDTH
C_GROUP = 16
C_GROUPS = 64
C_STATE = 64
C_MIN_NEG_RE = -1e-4
MEM_LEN = 256
X_HEADS = 4
X_HD = 256
D_FF = 2816

TM = 512
FF_CHUNKS = (1280, 1536)
TAIL_SUBTILES = 2
VMEM_LIMIT = 56 * 1024 * 1024

NT_DIMS = (((1,), (1,)), ((), ()))
TN_DIMS = (((0,), (0,)), ((), ()))


def _cparams(n_axes):
    return pltpu.CompilerParams(
        dimension_semantics=("arbitrary",) * n_axes, vmem_limit_bytes=VMEM_LIMIT)


def _resident(shape):
    zeros = (0,) * len(shape)
    return pl.BlockSpec(shape, lambda *_: zeros, pipeline_mode=pl.Buffered(1))


def _rms(xf, w):
    return xf * lax.rsqrt(jnp.mean(xf * xf, axis=-1, keepdims=True) + NORM_EPS) * w


def _dot(a, b):
    return jnp.dot(a, b, preferred_element_type=F32)


def _proj_ab_kernel(x_ref, nw_ref, w_ref, cos_ref, sin_ref,
                    qa_ref, fa_ref, ia_ref, ga_ref, *b_refs_and_scratch):
    b_refs, (stage_s, stage4_s) = b_refs_and_scratch[:-2], b_refs_and_scratch[-2:]
    tm = x_ref.shape[0]
    h = _rms(x_ref[...], nw_ref[...]).astype(BF16)
    cos = cos_ref[...]
    sin = sin_ref[...]
    def write_b(j, z):
        for hd in range(4):
            zh = z[:, hd * 128:(hd + 1) * 128]
            if j < 2:
                zh = zh * cos + pltpu.roll(zh, 64, 1) * sin
            if j == 0:
                zh = zh * (B_HD ** -0.5)
            base = (j * 4 + hd) * tm
            o1_ref, o4_ref, o16_ref = (b_refs[3 * d_idx + j] for d_idx in range(3))
            o1_ref[hd] = zh.astype(o1_ref.dtype)
            stage_s[base:base + tm, :] = zh
            for r4 in range(4):
                cls = stage_s[pl.ds(base + r4, tm // 4, stride=4), :]
                o4_ref[hd, :, r4 * 128:(r4 + 1) * 128] = cls.astype(o4_ref.dtype)
                cbase = base + r4 * (tm // 4)
                stage4_s[cbase:cbase + tm // 4, :] = cls
                for w in range(4):
                    r = r4 + 4 * w
                    sub = stage4_s[pl.ds(cbase + w, tm // 16, stride=4), :]
                    o16_ref[hd, :, r * 128:(r + 1) * 128] = sub.astype(o16_ref.dtype)

    def write_a(o_ref, z):
        for hd in range(4):
            o_ref[hd] = z[:, hd * 128:(hd + 1) * 128].astype(o_ref.dtype)

    jobs = [(4 + j, functools.partial(write_b, j)) for j in range(3)]
    jobs += [(j, functools.partial(write_a, o_ref))
             for j, o_ref in enumerate((qa_ref, fa_ref, ia_ref, ga_ref))]
    pending = None
    for c, write in jobs:
        z = _dot(h, w_ref[:, c * 512:(c + 1) * 512])
        if pending is not None:
            pending[0](pending[1])
        pending = (write, z)
    pending[0](pending[1])


def _proj_ab(x2, nw, w_in, cos_t, sin_t, bsz, seq):
    n = x2.shape[0]
    tiles = seq // TM
    head_out = lambda dt: jax.ShapeDtypeStruct((4, n, 128), dt)
    head_spec = pl.BlockSpec((4, TM, 128), lambda i: (0, i, 0))
    view_out = lambda d: jax.ShapeDtypeStruct((4, bsz, seq // d, d * 128), BF16)
    view_spec = lambda d: pl.BlockSpec((4, None, TM // d, d * 128),
                                       lambda i: (0, i // tiles, i % tiles, 0))
    return pl.pallas_call(
        _proj_ab_kernel,
        grid=(n // TM,),
        in_specs=[
            pl.BlockSpec((TM, D_MODEL), lambda i: (i, 0)),
            _resident((1, D_MODEL)),
            _resident((D_MODEL, IN_AB_WIDTH)),
            pl.BlockSpec((TM, 128), lambda i: (i % tiles, 0)),
            pl.BlockSpec((TM, 128), lambda i: (i % tiles, 0)),
        ],
        out_specs=[head_spec] * 4 + [view_spec(d) for d in B_DILS for _ in range(3)],
        out_shape=([head_out(BF16), head_out(F32), head_out(BF16), head_out(BF16)]
                   + [view_out(d) for d in B_DILS for _ in range(3)]),
        scratch_shapes=[pltpu.VMEM((12 * TM, 128), F32)] * 2,
        compiler_params=_cparams(1),
        name="proj_ab",
    )(x2, nw, w_in, cos_t, sin_t)


HG_BLK = 256


def _hgrn_stages(q_ref, f_ref, i_ref, g_ref, lb_ref, onw_ref, o_ref,
                 qdec_s, dec_s, oacc_s, upd_s, state_s):
    seq = q_ref.shape[0]
    lb = lb_ref[...]
    ri = lax.broadcasted_iota(jnp.int32, (2 * HG_BLK, HG_BLK), 0)
    ci = lax.broadcasted_iota(jnp.int32, (2 * HG_BLK, HG_BLK), 1)
    rr = jnp.where(ri >= HG_BLK, ri - HG_BLK, ri)
    same = (rr // A_CHUNK) == (ci // A_CHUNK)
    lmat = jnp.where(same & ((ci <= rr) | (ri >= HG_BLK)), 1.0, 0.0).astype(BF16)
    causal = (same & (ci <= rr))[:HG_BLK]

    blocks = [pl.ds(blk * HG_BLK, HG_BLK) for blk in range(seq // HG_BLK)]
    chunks_per_blk = HG_BLK // A_CHUNK
    log_f, k = [], []
    for rows in blocks:
        f = lb + (1.0 - lb) * jax.nn.sigmoid(f_ref[rows, :])
        log_f.append(jnp.log(f))
        k.append(1.0 - f)
    yield
    bb = []
    for lf in log_f:
        hi = lf.astype(BF16)
        lo = (lf - hi.astype(F32)).astype(BF16)
        bb.append(_dot(lmat, hi) + _dot(lmat, lo))
    yield
    q_dec, k_inv, k_end = [], [], []
    for rows, bbg, kg in zip(blocks, bb, k):
        b, b_last = bbg[:HG_BLK], bbg[HG_BLK:]
        q_dec.append((q_ref[rows, :].astype(F32) * jnp.exp(b)).astype(BF16))
        k_inv.append((kg * jnp.exp(-b)).astype(BF16))
        k_end.append((kg * jnp.exp(b_last - b)).astype(BF16))
        qdec_s[rows, :] = q_dec[-1]
        dec_s[rows, :] = jnp.exp(b_last)
    yield
    scores = [lax.dot_general(qg, kg, NT_DIMS, preferred_element_type=F32)
              for qg, kg in zip(q_dec, k_inv)]
    yield
    for blk, (rows, keg) in enumerate(zip(blocks, k_end)):
        v_blk = i_ref[rows, :]
        for c in range(chunks_per_blk):
            crow = slice(c * A_CHUNK, (c + 1) * A_CHUNK)
            upd_s[blk * chunks_per_blk + c] = lax.dot_general(
                v_blk[crow], keg[crow], TN_DIMS, preferred_element_type=F32)
        if blk % 2 == 1:
            yield
    for rows, sg in zip(blocks, scores):
        oacc_s[rows, :] = _dot(jnp.where(causal, sg, 0.0).astype(BF16), i_ref[rows, :])
    yield

    state_t = jnp.zeros((A_DK, A_DK), F32)
    for blk, rows in enumerate(blocks):
        parts = []
        for c in range(chunks_per_blk):
            n = blk * chunks_per_blk + c
            state_s[n] = state_t.astype(BF16)
            state_t = dec_s[n * A_CHUNK:n * A_CHUNK + 1, :] * state_t + upd_s[n]
        yield
        for c in range(chunks_per_blk):
            n = blk * chunks_per_blk + c
            parts.append(lax.dot_general(qdec_s[n * A_CHUNK:(n + 1) * A_CHUNK, :], state_s[n],
                                         NT_DIMS, preferred_element_type=F32))
        o = oacc_s[rows, :] + jnp.concatenate(parts, axis=0)
        o = o * lax.rsqrt(jnp.mean(o * o, axis=-1, keepdims=True) + NORM_EPS) * onw_ref[...]
        g = g_ref[rows, :].astype(F32)
        o_ref[rows, :] = (o * (g * jax.nn.sigmoid(g))).astype(o_ref.dtype)
        yield


DIL_GROUP = 8


def _dil_stages(q1_ref, k1_ref, v1_ref, q4_ref, k4_ref, v4_ref, q16_ref, k16_ref, v16_ref,
                o_ref, acc_c, m_c, l_c, acc_n, m_n, l_n):
    seq = q1_ref.shape[0]
    cls4 = seq // 4
    views = {1: (q1_ref, k1_ref, v1_ref), 4: (q4_ref, k4_ref, v4_ref),
             16: (q16_ref, k16_ref, v16_ref)}

    iq = lax.broadcasted_iota(jnp.int32, (B_SPAN, 2 * B_SPAN), 0)
    ik = lax.broadcasted_iota(jnp.int32, (B_SPAN, 2 * B_SPAN), 1)
    mask_first = (ik <= iq)[:, :B_SPAN]
    mask_win = (ik >= iq) & (ik <= iq + B_SPAN)

    def run_branch(dil, state, state_rows, emit):
        q_ref, k_ref, v_ref = views[dil]
        n_blk = seq // (dil * B_SPAN)
        todo = [(r, blk) for r in range(dil) for blk in range(n_blk)]
        for g0 in range(0, len(todo), DIL_GROUP):
            group = todo[g0:g0 + DIL_GROUP]
            scores = []
            for r, blk in group:
                lanes = slice(r * 128, (r + 1) * 128)
                k_rows = slice(max(blk - 1, 0) * B_SPAN, (blk + 1) * B_SPAN)
                s = lax.dot_general(q_ref[blk * B_SPAN:(blk + 1) * B_SPAN, lanes],
                                    k_ref[k_rows, lanes], NT_DIMS, preferred_element_type=F32)
                scores.append(jnp.where(mask_first if blk == 0 else mask_win, s, -jnp.inf))
            yield
            probs, stats = [], []
            for (r, blk), s in zip(group, scores):
                m_new = jnp.broadcast_to(jnp.max(s, axis=-1, keepdims=True), (B_SPAN, 128))
                alpha = None
                if state is not None:
                    m_old = state[1][state_rows(r, blk), :]
                    m_new = jnp.maximum(m_old, m_new)
                    alpha = jnp.exp(m_old - m_new)
                p = jnp.exp(s - jnp.concatenate([m_new] * (s.shape[1] // 128), axis=1))
                probs.append(p.astype(BF16))
                stats.append((m_new, alpha, jnp.sum(p, axis=-1, keepdims=True)))
            yield
            for (r, blk), p, (m_new, alpha, l_blk) in zip(group, probs, stats):
                lanes = slice(r * 128, (r + 1) * 128)
                k_rows = slice(max(blk - 1, 0) * B_SPAN, (blk + 1) * B_SPAN)
                acc = _dot(p, v_ref[k_rows, lanes])
                l_new = jnp.broadcast_to(l_blk, (B_SPAN, 128))
                if state is not None:
                    rows = state_rows(r, blk)
                    acc = alpha * state[0][rows, :] + acc
                    l_new = alpha * state[2][rows, :] + l_new
                emit(r, blk, acc, m_new, l_new)
            yield

    def store_to(refs, rows_of):
        def emit(r, blk, acc, m_new, l_new):
            rows = rows_of(r, blk)
            refs[0][rows, :] = acc
            refs[1][rows, :] = m_new
            refs[2][rows, :] = l_new
        return emit

    class_major = (acc_c, m_c, l_c)
    natural = (acc_n, m_n, l_n)
    rows16 = lambda r, blk: pl.ds((r % 4) * cls4 + r // 4, B_SPAN, stride=4)
    rows4 = lambda r, blk: pl.ds(r * cls4 + blk * B_SPAN, B_SPAN)
    yield from run_branch(16, None, None, store_to(class_major, rows16))
    yield from run_branch(4, class_major, rows4, store_to(class_major, rows4))
    for src, dst in zip(class_major, natural):
        for r in range(4):
            dst[pl.ds(r, cls4, stride=4), :] = src[r * cls4:(r + 1) * cls4, :]
    yield
    rows1 = lambda r, blk: pl.ds(blk * B_SPAN, B_SPAN)

    def emit_output(r, blk, acc, m_new, l_new):
        o_ref[rows1(r, blk), :] = (acc / l_new).astype(o_ref.dtype)

    yield from run_branch(1, natural, rows1, emit_output)


N_HGRN_IN, N_DIL_IN, N_HGRN_SCRATCH = 6, 9, 5


def _mixers_kernel(*refs):
    hgrn_in, refs = refs[:N_HGRN_IN], refs[N_HGRN_IN:]
    dil_in, refs = refs[:N_DIL_IN], refs[N_DIL_IN:]
    oa_ref, ob_ref = refs[:2]
    hgrn_scratch, dil_scratch = refs[2:2 + N_HGRN_SCRATCH], refs[2 + N_HGRN_SCRATCH:]
    pending = [_hgrn_stages(*hgrn_in, oa_ref, *hgrn_scratch),
               _dil_stages(*dil_in, ob_ref, *dil_scratch)]
    while pending:
        for stages in list(pending):
            if next(stages, StopIteration) is StopIteration:
                pending.remove(stages)


def _mixers(qa, fa, ia, ga, lb, onw, qkv_views, bsz, seq):
    view = lambda t: t.reshape(4, bsz, seq, 128)
    head_spec = pl.BlockSpec((None, None, seq, 128), lambda b, h: (h, b, 0, 0))
    vec_spec = pl.BlockSpec((None, 1, 128), lambda b, h: (h, 0, 0))
    dil_spec = lambda d: pl.BlockSpec((None, None, seq // d, d * 128), lambda b, h: (h, b, 0, 0))
    out_spec = pl.BlockSpec((None, seq, 128), lambda b, h: (b, 0, h))
    return pl.pallas_call(
        _mixers_kernel,
        grid=(bsz, A_HEADS),
        in_specs=([head_spec] * 4 + [vec_spec] * 2
                  + [dil_spec(d) for d in B_DILS for _ in range(3)]),
        out_specs=[out_spec, out_spec],
        out_shape=[jax.ShapeDtypeStruct((bsz, seq, A_WIDTH), BF16),
                   jax.ShapeDtypeStruct((bsz, seq, B_WIDTH), BF16)],
        scratch_shapes=([pltpu.VMEM((seq, 128), BF16),
                         pltpu.VMEM((seq, 128), F32), pltpu.VMEM((seq, 128), F32),
                         pltpu.VMEM((seq // A_CHUNK, A_DK, A_DK), F32),
                         pltpu.VMEM((seq // A_CHUNK, A_DK, A_DK), BF16)]
                        + [pltpu.VMEM((seq, 128), F32)] * 6),
        compiler_params=_cparams(2),
        name="mixers_ab",
    )(view(qa), view(fa), view(ia), view(ga), lb.reshape(4, 1, 128), onw.reshape(4, 1, 128),
      *qkv_views)


KV_BATCH = 4


def _kv_kernel(mem_ref, nw_ref, w_ref, k_ref, v_ref):
    rows = mem_ref.shape[0] * mem_ref.shape[1]
    h = _rms(mem_ref[...].reshape(rows, D_MODEL), nw_ref[...]).astype(BF16)
    k_ref[...] = _dot(h, w_ref[:, :D_MODEL]).astype(BF16).reshape(k_ref.shape)
    v_ref[...] = _dot(h, w_ref[:, D_MODEL:]).astype(BF16).reshape(v_ref.shape)


def _kv_proj(mem, nw, wkv):
    bsz = mem.shape[0]
    blk = pl.BlockSpec((KV_BATCH, MEM_LEN, D_MODEL), lambda b: (b, 0, 0))
    kv = jax.ShapeDtypeStruct((bsz, MEM_LEN, D_MODEL), BF16)
    return pl.pallas_call(
        _kv_kernel,
        grid=(bsz // KV_BATCH,),
        in_specs=[blk, _resident((1, D_MODEL)), _resident((D_MODEL, 2 * D_MODEL))],
        out_specs=[blk, blk],
        out_shape=[kv, kv],
        compiler_params=_cparams(1),
        name="xattn_kv",
    )(mem, nw, wkv)


def _tail_kernel(*refs, with_mixer_proj):
    if with_mixer_proj:
        oa_ref, ob_ref, w_mix_ref, nw_mix_ref = refs[:4]
        refs = refs[4:]
    (x_ref, k_ref, v_ref, wq_ref, wo_ref, nw2_ref, nw3_ref,
     nw4_ref, w_in_ref, w_out_ref, nw5_ref, o_ref) = refs
    sub = x_ref.shape[0] // TAIL_SUBTILES
    groups = [slice(g * sub, (g + 1) * sub) for g in range(TAIL_SUBTILES)]
    each = lambda fn, *lists: [fn(*args) for args in zip(*lists)]

    x = [x_ref[rows, :] for rows in groups]
    if with_mixer_proj:
        y = [_dot(oa_ref[rows, :], w_mix_ref[:A_WIDTH, :])
             + _dot(ob_ref[rows, :], w_mix_ref[A_WIDTH:, :]) for rows in groups]
        x = each(lambda xg, yg: xg + _rms(yg, nw_mix_ref[...]), x, y)

    h = each(lambda xg: _rms(xg, nw2_ref[...]).astype(BF16), x)
    q = each(lambda hg: _dot(hg, wq_ref[...]), h)
    heads = [[] for _ in groups]
    for hd in range(X_HEADS):
        cols = slice(hd * X_HD, (hd + 1) * X_HD)
        s = each(lambda qg: lax.dot_general((qg[:, cols] * (X_HD ** -0.5)).astype(BF16),
                                            k_ref[:, cols], NT_DIMS, preferred_element_type=F32), q)
        p = each(lambda sg: jnp.exp(sg - jnp.max(sg, axis=-1, keepdims=True)), s)
        pv = each(lambda pg: _dot(pg.astype(BF16), v_ref[:, cols]), p)
        for g, (pg, pvg) in enumerate(zip(p, pv)):
            heads[g].append((pvg / jnp.sum(pg, axis=-1, keepdims=True)).astype(BF16))
    y = each(lambda hg: _dot(jnp.concatenate(hg, axis=1), wo_ref[...]), heads)
    x = each(lambda xg, yg: xg + _rms(yg, nw3_ref[...]), x, y)

    h = each(lambda xg: _rms(xg, nw4_ref[...]).astype(BF16), x)
    y = [None] * len(groups)
    lo = 0
    for width in FF_CHUNKS:
        gate, up = slice(lo, lo + width), slice(D_FF + lo, D_FF + lo + width)
        z1 = each(lambda hg: _dot(hg, w_in_ref[:, gate]), h)
        z2 = each(lambda hg: _dot(hg, w_in_ref[:, up]), h)
        a = each(lambda z1g, z2g: (z1g * jax.nn.sigmoid(z1g) * z2g).astype(BF16), z1, z2)
        part = each(lambda ag: _dot(ag, w_out_ref[lo:lo + width, :]), a)
        y = [pg if yg is None else yg + pg for yg, pg in zip(y, part)]
        lo += width
    for rows, xg, yg in zip(groups, x, y):
        o_ref[rows, :] = xg + _rms(yg, nw5_ref[...])


def _row_spec(time_major, seq):
    if time_major:
        return pl.BlockSpec((TM, D_MODEL), lambda b, i: (i, b))
    tiles = seq // TM
    return pl.BlockSpec((TM, D_MODEL), lambda b, i: (b * tiles + i, 0))


def _stream_shape(time_major, bsz, seq):
    shape = (seq, bsz * D_MODEL) if time_major else (bsz * seq, D_MODEL)
    return jax.ShapeDtypeStruct(shape, F32)


def _layer_tail(xs, mixer, k, v, wq, wo, nw2, nw3, nw4, w_in, w_out, nw5, bsz, seq,
                in_time_major, out_time_major):
    tiles = seq // TM
    vec = _resident((1, D_MODEL))
    kvs = pl.BlockSpec((None, MEM_LEN, D_MODEL), lambda b, i: (b, 0, 0))
    in_specs = [_row_spec(in_time_major, seq), kvs, kvs,
                _resident((D_MODEL, D_MODEL)), _resident((D_MODEL, D_MODEL)), vec, vec,
                vec, _resident((D_MODEL, 2 * D_FF)), _resident((D_FF, D_MODEL)), vec]
    operands = [xs, k, v, wq, wo, nw2, nw3, nw4, w_in, w_out, nw5]
    if mixer is not None:
        half = pl.BlockSpec((TM, A_WIDTH), lambda b, i: (b * tiles + i, 0))
        in_specs = [half, half, _resident((D_MODEL, D_MODEL)), vec] + in_specs
        operands = list(mixer) + operands
    return pl.pallas_call(
        functools.partial(_tail_kernel, with_mixer_proj=mixer is not None),
        grid=(bsz, tiles),
        in_specs=in_specs,
        out_specs=_row_spec(out_time_major, seq),
        out_shape=_stream_shape(out_time_major, bsz, seq),
        compiler_params=_cparams(2),
        name="layer_tail",
    )(*operands)


S5_NB = 8
S5_TT = 64
S5_CB = 4
S5_BLK_STATE = 16 * 2 * C_STATE
S5_PAIRS = 8


def _s5_kernel(x_ref, nwa_ref, bt_ref, ct_ref, a_ref, d_ref, wg_ref, nwb_ref, o_ref,
               xs_s, v_s, h_s, y_s, st_s, os_s):
    tt = x_ref.shape[0]
    rows = tt * S5_NB
    lane_slabs = D_MODEL // 128

    @pl.when(pl.program_id(1) == 0)
    def _():
        st_s[...] = jnp.zeros_like(st_s)

    for b in range(S5_NB):
        for k in range(lane_slabs):
            lanes = slice(b * D_MODEL + k * 128, b * D_MODEL + (k + 1) * 128)
            xs_s[pl.ds(k * rows + b, tt, stride=S5_NB), :] = x_ref[:, lanes]

    halves = [slice(0, rows // 2), slice(rows // 2, rows)]

    def load_x(hv):
        return jnp.concatenate([xs_s[k * rows + hv.start:k * rows + hv.stop, :]
                                for k in range(lane_slabs)], axis=1)

    ub = _rms(load_x(slice(0, rows)), nwa_ref[...]).astype(BF16)

    def project(cb):
        v_s[cb % 2] = _dot(ub[:, cb * 256:(cb + 1) * 256], bt_ref[cb])

    project(0)
    for cb in range(S5_CB):
        vb, hb = v_s.at[cb % 2], h_s.at[cb % 2]
        if cb + 1 < S5_CB:
            project(cb + 1)
        for m in range(S5_PAIRS):
            ar = jnp.broadcast_to(a_ref[cb, m:m + 1, :], (S5_NB, 128))
            ai = jnp.broadcast_to(a_ref[cb, S5_PAIRS + m:S5_PAIRS + m + 1, :], (S5_NB, 128))
            c_re = slice(256 * m, 256 * m + 128)
            c_im = slice(256 * m + 128, 256 * m + 256)
            s_re = slice(cb * S5_BLK_STATE + 256 * m, cb * S5_BLK_STATE + 256 * m + 128)
            s_im = slice(cb * S5_BLK_STATE + 256 * m + 128, cb * S5_BLK_STATE + 256 * m + 256)
            hr, hi = st_s[:, s_re], st_s[:, s_im]
            for i in range(tt // 2):
                hist_r, hist_i = [], []
                for step in range(2):
                    rows_t = slice((2 * i + step) * S5_NB, (2 * i + step + 1) * S5_NB)
                    hr, hi = (ar * hr - ai * hi + vb[rows_t, c_re],
                              ar * hi + ai * hr + vb[rows_t, c_im])
                    hist_r.append(hr)
                    hist_i.append(hi)
                both = slice(2 * i * S5_NB, (2 * i + 2) * S5_NB)
                hb[both, c_re] = jnp.concatenate(hist_r, axis=0).astype(BF16)
                hb[both, c_im] = jnp.concatenate(hist_i, axis=0).astype(BF16)
            st_s[:, s_re] = hr
            st_s[:, s_im] = hi
        y_s[:, cb * 256:(cb + 1) * 256] = _dot(hb[...], ct_ref[cb])

    x = [load_x(hv) for hv in halves]
    gl = []
    for hv, xh in zip(halves, x):
        t = y_s[hv, :] + d_ref[...] * _rms(xh, nwa_ref[...])
        gl.append((0.5 * t * (1.0 + lax.erf(t * math.sqrt(0.5)))).astype(BF16))
    z = [_dot(g, wg_ref[...]) for g in gl]
    for hv, xh, zh in zip(halves, x, z):
        y = zh[:, :D_MODEL] * jax.nn.sigmoid(zh[:, D_MODEL:])
        out = xh + _rms(y, nwb_ref[...])
        for k in range(lane_slabs):
            os_s[k * rows + hv.start:k * rows + hv.stop, :] = out[:, k * 128:(k + 1) * 128]
    for b in range(S5_NB):
        for k in range(lane_slabs):
            lanes = slice(b * D_MODEL + k * 128, b * D_MODEL + (k + 1) * 128)
            o_ref[:, lanes] = os_s[pl.ds(k * rows + b, tt, stride=S5_NB), :]


def _s5_block(xs, nwa, bt, ct, avec, d, w_glu, nwb, bsz, seq, tt=S5_TT):
    rows = tt * S5_NB
    tile = pl.BlockSpec((tt, S5_NB * D_MODEL), lambda g, i: (i, g))
    return pl.pallas_call(
        _s5_kernel,
        grid=(bsz // S5_NB, seq // tt),
        in_specs=[tile, _resident((1, D_MODEL)), _resident(bt.shape), _resident(ct.shape),
                  _resident(avec.shape), _resident((1, D_MODEL)),
                  _resident((D_MODEL, 2 * D_MODEL)), _resident((1, D_MODEL))],
        out_specs=tile,
        out_shape=_stream_shape(True, bsz, seq),
        scratch_shapes=[pltpu.VMEM((D_MODEL // 128 * rows, 128), F32),
                        pltpu.VMEM((2, rows, S5_BLK_STATE), F32),
                        pltpu.VMEM((2, rows, S5_BLK_STATE), BF16),
                        pltpu.VMEM((rows, D_MODEL), F32),
                        pltpu.VMEM((S5_NB, S5_CB * S5_BLK_STATE), F32),
                        pltpu.VMEM((D_MODEL // 128 * rows, 128), F32)],
        compiler_params=_cparams(2),
        name="s5_block",
    )(xs, nwa, bt, ct, avec, d, w_glu, nwb)


def _s5_weights(lam_re, lam_im, log_dt, b_re, b_im, c_re, c_im):
    lr = jnp.minimum(lam_re.astype(F32), C_MIN_NEG_RE)
    li = lam_im.astype(F32)
    dt = jnp.exp(log_dt.astype(F32))[:, None]
    mag = jnp.exp(dt * lr)
    ar, ai = mag * jnp.cos(dt * li), mag * jnp.sin(dt * li)
    den = lr * lr + li * li
    zr = ((ar - 1.0) * lr + ai * li) / den
    zi = (ai * lr - (ar - 1.0) * li) / den
    br, bi = b_re.astype(F32), b_im.astype(F32)
    bbr = zr[..., None] * br - zi[..., None] * bi
    bbi = zr[..., None] * bi + zi[..., None] * br
    cr, ci = c_re.astype(F32), c_im.astype(F32)
    split = lambda t: t.reshape((S5_CB, S5_PAIRS, 2) + t.shape[1:])
    chan_group = jnp.arange(256) // C_GROUP
    col = jnp.arange(S5_BLK_STATE)
    state_group = (col // 256) * 2 + (col // C_STATE) % 2
    same_group = chan_group[:, None] == state_group[None, :]
    bb = jnp.stack([split(bbr), split(bbi)], axis=0)
    strip = bb.transpose(1, 5, 2, 0, 3, 4).reshape(S5_CB, C_GROUP, S5_BLK_STATE)
    bt = jnp.where(same_group[None], jnp.tile(strip, (1, 256 // C_GROUP, 1)), 0.0)
    cc = jnp.stack([split(cr), -split(ci)], axis=0)
    strip = cc.transpose(1, 2, 0, 3, 5, 4).reshape(S5_CB, S5_BLK_STATE, C_GROUP)
    ct = jnp.where(same_group.T[None], jnp.tile(strip, (1, 1, 256 // C_GROUP)), 0.0)
    lanes = lambda t: t.reshape(S5_CB, S5_PAIRS, 2 * C_STATE)
    avec = jnp.concatenate([lanes(ar), lanes(ai)], axis=1)
    return bt.astype(BF16), ct.astype(BF16), avec


def _rotary_tables(seq):
    half = B_HD // 2
    inv_freq = ROPE_THETA ** (-jnp.arange(half, dtype=F32) / half)
    ang = jnp.arange(seq, dtype=jnp.int32).astype(F32)[:, None] * inv_freq[None, :]
    cos, sin = jnp.cos(ang), jnp.sin(ang)
    return jnp.concatenate([cos, cos], axis=-1), jnp.concatenate([-sin, sin], axis=-1)


def kernel(x, mem, norm_w, mem_norm_w, ab_w_in, ab_w_out, hgrn_lb_logits, hgrn_out_norm_w, s5_lambda_re, s5_lambda_im, s5_log_dt, s5_b_re, s5_b_im, s5_c_re, s5_c_im, s5_d, s5_w_glu, xattn_wq, xattn_wkv, xattn_wo, ffn_w_in, ffn_w_out):
    bsz, seq, _ = x.shape
    n = bsz * seq
    depth = norm_w.shape[0]
    nw = lambda layer, i: norm_w[layer, i].astype(F32).reshape(1, D_MODEL)
    lb_table = jnp.cumsum(jax.nn.softmax(hgrn_lb_logits.astype(F32), axis=0), axis=0)
    cos_t, sin_t = _rotary_tables(seq)
    xs = x.reshape(n, D_MODEL)
    for layer in range(depth):
        j = layer // 2
        time_major = layer % 2 == 1
        mixer = None
        if not time_major:
            outs = _proj_ab(xs, nw(layer, 0), ab_w_in[j].astype(BF16), cos_t, sin_t, bsz, seq)
            qa, fa, ia, ga = outs[:4]
            oa, ob = _mixers(qa, fa, ia, ga, lb_table[layer], hgrn_out_norm_w[j].astype(F32),
                             outs[4:], bsz, seq)
            mixer = (oa.reshape(n, A_WIDTH), ob.reshape(n, B_WIDTH),
                     ab_w_out[j].astype(BF16), nw(layer, 1))
        else:
            bt, ct, avec = _s5_weights(s5_lambda_re[j], s5_lambda_im[j], s5_log_dt[j],
                                       s5_b_re[j], s5_b_im[j], s5_c_re[j], s5_c_im[j])
            xs = _s5_block(xs, nw(layer, 0), bt, ct, avec,
                           s5_d[j].astype(F32).reshape(1, D_MODEL), s5_w_glu[j].astype(BF16),
                           nw(layer, 1), bsz, seq)
        k, v = _kv_proj(mem, mem_norm_w[layer].astype(F32).reshape(1, D_MODEL),
                        xattn_wkv[layer].astype(BF16))
        next_time_major = layer + 1 < depth and (layer + 1) % 2 == 1
        xs = _layer_tail(xs, mixer, k, v, xattn_wq[layer].astype(BF16),
                         xattn_wo[layer].astype(BF16), nw(layer, 2), nw(layer, 3), nw(layer, 4),
                         ffn_w_in[layer].astype(BF16), ffn_w_out[layer].astype(BF16),
                         nw(layer, 5), bsz, seq, time_major, next_time_major)
    return xs.reshape(bsz, seq, D_MODEL)
```

```python
import functools
import math

import jax
import jax.numpy as jnp
from jax import lax
from jax.experimental import pallas as pl
from jax.experimental.pallas import tpu as pltpu

F32 = jnp.float32
BF16 = jnp.bfloat16

D_MODEL = 1024
NORM_EPS = 1e-6
A_HEADS = 4
A_DK = 128
A_WIDTH = 512
A_CHUNK = 32
B_HEADS = 4
B_HD = 128
B_WIDTH = 512
B_SPAN = 128
B_DILS = (1, 4, 16)
ROPE_THETA = 10000.0
IN_AB_WIDTH = 4 * A_WIDTH + 3 * B_WIDTH
C_GROUP = 16
C_GROUPS = 64
C_STATE = 64
C_MIN_NEG_RE = -1e-4
MEM_LEN = 256
X_HEADS = 4
X_HD = 256
D_FF = 2816

TM = 512
FF_CHUNKS = (1280, 1536)
TAIL_SUBTILES = 2
VMEM_LIMIT = 56 * 1024 * 1024

NT_DIMS = (((1,), (1,)), ((), ()))
TN_DIMS = (((0,), (0,)), ((), ()))


def _cparams(n_axes):
    return pltpu.CompilerParams(
        dimension_semantics=("arbitrary",) * n_axes, vmem_limit_bytes=VMEM_LIMIT)


def _resident(shape):
    zeros = (0,) * len(shape)
    return pl.BlockSpec(shape, lambda *_: zeros, pipeline_mode=pl.Buffered(1))


def _rms(xf, w):
    return xf * lax.rsqrt(jnp.mean(xf * xf, axis=-1, keepdims=True) + NORM_EPS) * w


def _dot(a, b):
    return jnp.dot(a, b, preferred_element_type=F32)


def _proj_ab_kernel(x_ref, nw_ref, w_ref, cos_ref, sin_ref,
                    qa_ref, fa_ref, ia_ref, ga_ref, *b_refs_and_scratch):
    b_refs, (stage_s, stage4_s) = b_refs_and_scratch[:-2], b_refs_and_scratch[-2:]
    tm = x_ref.shape[0]
    h = _rms(x_ref[...], nw_ref[...]).astype(BF16)
    cos = cos_ref[...]
    sin = sin_ref[...]
    def write_b(j, z):
        for hd in range(4):
            zh = z[:, hd * 128:(hd + 1) * 128]
            if j < 2:
                zh = zh * cos + pltpu.roll(zh, 64, 1) * sin
            if j == 0:
                zh = zh * (B_HD ** -0.5)
            base = (j * 4 + hd) * tm
            o1_ref, o4_ref, o16_ref = (b_refs[3 * d_idx + j] for d_idx in range(3))
            o1_ref[hd] = zh.astype(o1_ref.dtype)
            stage_s[base:base + tm, :] = zh
            for r4 in range(4):
                cls = stage_s[pl.ds(base + r4, tm // 4, stride=4), :]
                o4_ref[hd, :, r4 * 128:(r4 + 1) * 128] = cls.astype(o4_ref.dtype)
                cbase = base + r4 * (tm // 4)
                stage4_s[cbase:cbase + tm // 4, :] = cls
                for w in range(4):
                    r = r4 + 4 * w
                    sub = stage4_s[pl.ds(cbase + w, tm // 16, stride=4), :]
                    o16_ref[hd, :, r * 128:(r + 1) * 128] = sub.astype(o16_ref.dtype)

    def write_a(o_ref, z):
        for hd in range(4):
            o_ref[hd] = z[:, hd * 128:(hd + 1) * 128].astype(o_ref.dtype)

    jobs = [(4 + j, functools.partial(write_b, j)) for j in range(3)]
    jobs += [(j, functools.partial(write_a, o_ref))
             for j, o_ref in enumerate((qa_ref, fa_ref, ia_ref, ga_ref))]
    pending = None
    for c, write in jobs:
        z = _dot(h, w_ref[:, c * 512:(c + 1) * 512])
        if pending is not None:
            pending[0](pending[1])
        pending = (write, z)
    pending[0](pending[1])


def _proj_ab(x2, nw, w_in, cos_t, sin_t, bsz, seq):
    n = x2.shape[0]
    tiles = seq // TM
    head_out = lambda dt: jax.ShapeDtypeStruct((4, n, 128), dt)
    head_spec = pl.BlockSpec((4, TM, 128), lambda i: (0, i, 0))
    view_out = lambda d: jax.ShapeDtypeStruct((4, bsz, seq // d, d * 128), BF16)
    view_spec = lambda d: pl.BlockSpec((4, None, TM // d, d * 128),
                                       lambda i: (0, i // tiles, i % tiles, 0))
    return pl.pallas_call(
        _proj_ab_kernel,
        grid=(n // TM,),
        in_specs=[
            pl.BlockSpec((TM, D_MODEL), lambda i: (i, 0)),
            _resident((1, D_MODEL)),
            _resident((D_MODEL, IN_AB_WIDTH)),
            pl.BlockSpec((TM, 128), lambda i: (i % tiles, 0)),
            pl.BlockSpec((TM, 128), lambda i: (i % tiles, 0)),
        ],
        out_specs=[head_spec] * 4 + [view_spec(d) for d in B_DILS for _ in range(3)],
        out_shape=([head_out(BF16), head_out(F32), head_out(BF16), head_out(BF16)]
                   + [view_out(d) for d in B_DILS for _ in range(3)]),
        scratch_shapes=[pltpu.VMEM((12 * TM, 128), F32)] * 2,
        compiler_params=_cparams(1),
        name="proj_ab",
    )(x2, nw, w_in, cos_t, sin_t)


HG_BLK = 256


def _hgrn_stages(q_ref, f_ref, i_ref, g_ref, lb_ref, onw_ref, o_ref,
                 qdec_s, dec_s, oacc_s, upd_s, state_s):
    seq = q_ref.shape[0]
    lb = lb_ref[...]
    ri = lax.broadcasted_iota(jnp.int32, (2 * HG_BLK, HG_BLK), 0)
    ci = lax.broadcasted_iota(jnp.int32, (2 * HG_BLK, HG_BLK), 1)
    rr = jnp.where(ri >= HG_BLK, ri - HG_BLK, ri)
    same = (rr // A_CHUNK) == (ci // A_CHUNK)
    lmat = jnp.where(same & ((ci <= rr) | (ri >= HG_BLK)), 1.0, 0.0).astype(BF16)
    causal = (same & (ci <= rr))[:HG_BLK]

    blocks = [pl.ds(blk * HG_BLK, HG_BLK) for blk in range(seq // HG_BLK)]
    chunks_per_blk = HG_BLK // A_CHUNK
    log_f, k = [], []
    for rows in blocks:
        f = lb + (1.0 - lb) * jax.nn.sigmoid(f_ref[rows, :])
        log_f.append(jnp.log(f))
        k.append(1.0 - f)
    yield
    bb = []
    for lf in log_f:
        hi = lf.astype(BF16)
        lo = (lf - hi.astype(F32)).astype(BF16)
        bb.append(_dot(lmat, hi) + _dot(lmat, lo))
    yield
    q_dec, k_inv, k_end = [], [], []
    for rows, bbg, kg in zip(blocks, bb, k):
        b, b_last = bbg[:HG_BLK], bbg[HG_BLK:]
        q_dec.append((q_ref[rows, :].astype(F32) * jnp.exp(b)).astype(BF16))
        k_inv.append((kg * jnp.exp(-b)).astype(BF16))
        k_end.append((kg * jnp.exp(b_last - b)).astype(BF16))
        qdec_s[rows, :] = q_dec[-1]
        dec_s[rows, :] = jnp.exp(b_last)
    yield
    scores = [lax.dot_general(qg, kg, NT_DIMS, preferred_element_type=F32)
              for qg, kg in zip(q_dec, k_inv)]
    yield
    for blk, (rows, keg) in enumerate(zip(blocks, k_end)):
        v_blk = i_ref[rows, :]
        for c in range(chunks_per_blk):
            crow = slice(c * A_CHUNK, (c + 1) * A_CHUNK)
            upd_s[blk * chunks_per_blk + c] = lax.dot_general(
                v_blk[crow], keg[crow], TN_DIMS, preferred_element_type=F32)
        if blk % 2 == 1:
            yield
    for rows, sg in zip(blocks, scores):
        oacc_s[rows, :] = _dot(jnp.where(causal, sg, 0.0).astype(BF16), i_ref[rows, :])
    yield

    state_t = jnp.zeros((A_DK, A_DK), F32)
    for blk, rows in enumerate(blocks):
        parts = []
        for c in range(chunks_per_blk):
            n = blk * chunks_per_blk + c
            state_s[n] = state_t.astype(BF16)
            state_t = dec_s[n * A_CHUNK:n * A_CHUNK + 1, :] * state_t + upd_s[n]
        yield
        for c in range(chunks_per_blk):
            n = blk * chunks_per_blk + c
            parts.append(lax.dot_general(qdec_s[n * A_CHUNK:(n + 1) * A_CHUNK, :], state_s[n],
                                         NT_DIMS, preferred_element_type=F32))
        o = oacc_s[rows, :] + jnp.concatenate(parts, axis=0)
        o = o * lax.rsqrt(jnp.mean(o * o, axis=-1, keepdims=True) + NORM_EPS) * onw_ref[...]
        g = g_ref[rows, :].astype(F32)
        o_ref[rows, :] = (o * (g * jax.nn.sigmoid(g))).astype(o_ref.dtype)
        yield


DIL_GROUP = 8


def _dil_stages(q1_ref, k1_ref, v1_ref, q4_ref, k4_ref, v4_ref, q16_ref, k16_ref, v16_ref,
                o_ref, acc_c, m_c, l_c, acc_n, m_n, l_n):
    seq = q1_ref.shape[0]
    cls4 = seq // 4
    views = {1: (q1_ref, k1_ref, v1_ref), 4: (q4_ref, k4_ref, v4_ref),
             16: (q16_ref, k16_ref, v16_ref)}

    iq = lax.broadcasted_iota(jnp.int32, (B_SPAN, 2 * B_SPAN), 0)
    ik = lax.broadcasted_iota(jnp.int32, (B_SPAN, 2 * B_SPAN), 1)
    mask_first = (ik <= iq)[:, :B_SPAN]
    mask_win = (ik >= iq) & (ik <= iq + B_SPAN)

    def run_branch(dil, state, state_rows, emit):
        q_ref, k_ref, v_ref = views[dil]
        n_blk = seq // (dil * B_SPAN)
        todo = [(r, blk) for r in range(dil) for blk in range(n_blk)]
        for g0 in range(0, len(todo), DIL_GROUP):
            group = todo[g0:g0 + DIL_GROUP]
            scores = []
            for r, blk in group:
                lanes = slice(r * 128, (r + 1) * 128)
                k_rows = slice(max(blk - 1, 0) * B_SPAN, (blk + 1) * B_SPAN)
                s = lax.dot_general(q_ref[blk * B_SPAN:(blk + 1) * B_SPAN, lanes],
                                    k_ref[k_rows, lanes], NT_DIMS, preferred_element_type=F32)
                scores.append(jnp.where(mask_first if blk == 0 else mask_win, s, -jnp.inf))
            yield
            probs, stats = [], []
            for (r, blk), s in zip(group, scores):
                m_new = jnp.broadcast_to(jnp.max(s, axis=-1, keepdims=True), (B_SPAN, 128))
                alpha = None
                if state is not None:
                    m_old = state[1][state_rows(r, blk), :]
                    m_new = jnp.maximum(m_old, m_new)
                    alpha = jnp.exp(m_old - m_new)
                p = jnp.exp(s - jnp.concatenate([m_new] * (s.shape[1] // 128), axis=1))
                probs.append(p.astype(BF16))
                stats.append((m_new, alpha, jnp.sum(p, axis=-1, keepdims=True)))
            yield
            for (r, blk), p, (m_new, alpha, l_blk) in zip(group, probs, stats):
                lanes = slice(r * 128, (r + 1) * 128)
                k_rows = slice(max(blk - 1, 0) * B_SPAN, (blk + 1) * B_SPAN)
                acc = _dot(p, v_ref[k_rows, lanes])
                l_new = jnp.broadcast_to(l_blk, (B_SPAN, 128))
                if state is not None:
                    rows = state_rows(r, blk)
                    acc = alpha * state[0][rows, :] + acc
                    l_new = alpha * state[2][rows, :] + l_new
                emit(r, blk, acc, m_new, l_new)
            yield

    def store_to(refs, rows_of):
        def emit(r, blk, acc, m_new, l_new):
            rows = rows_of(r, blk)
            refs[0][rows, :] = acc
            refs[1][rows, :] = m_new
            refs[2][rows, :] = l_new
        return emit

    class_major = (acc_c, m_c, l_c)
    natural = (acc_n, m_n, l_n)
    rows16 = lambda r, blk: pl.ds((r % 4) * cls4 + r // 4, B_SPAN, stride=4)
    rows4 = lambda r, blk: pl.ds(r * cls4 + blk * B_SPAN, B_SPAN)
    yield from run_branch(16, None, None, store_to(class_major, rows16))
    yield from run_branch(4, class_major, rows4, store_to(class_major, rows4))
    for src, dst in zip(class_major, natural):
        for r in range(4):
            dst[pl.ds(r, cls4, stride=4), :] = src[r * cls4:(r + 1) * cls4, :]
    yield
    rows1 = lambda r, blk: pl.ds(blk * B_SPAN, B_SPAN)

    def emit_output(r, blk, acc, m_new, l_new):
        o_ref[rows1(r, blk), :] = (acc / l_new).astype(o_ref.dtype)

    yield from run_branch(1, natural, rows1, emit_output)


N_HGRN_IN, N_DIL_IN, N_HGRN_SCRATCH = 6, 9, 5


def _mixers_kernel(*refs):
    hgrn_in, refs = refs[:N_HGRN_IN], refs[N_HGRN_IN:]
    dil_in, refs = refs[:N_DIL_IN], refs[N_DIL_IN:]
    oa_ref, ob_ref = refs[:2]
    hgrn_scratch, dil_scratch = refs[2:2 + N_HGRN_SCRATCH], refs[2 + N_HGRN_SCRATCH:]
    pending = [_hgrn_stages(*hgrn_in, oa_ref, *hgrn_scratch),
               _dil_stages(*dil_in, ob_ref, *dil_scratch)]
    while pending:
        for stages in list(pending):
            if next(stages, StopIteration) is StopIteration:
                pending.remove(stages)


def _mixers(qa, fa, ia, ga, lb, onw, qkv_views, bsz, seq):
    view = lambda t: t.reshape(4, bsz, seq, 128)
    head_spec = pl.BlockSpec((None, None, seq, 128), lambda b, h: (h, b, 0, 0))
    vec_spec = pl.BlockSpec((None, 1, 128), lambda b, h: (h, 0, 0))
    dil_spec = lambda d: pl.BlockSpec((None, None, seq // d, d * 128), lambda b, h: (h, b, 0, 0))
    out_spec = pl.BlockSpec((None, seq, 128), lambda b, h: (b, 0, h))
    return pl.pallas_call(
        _mixers_kernel,
        grid=(bsz, A_HEADS),
        in_specs=([head_spec] * 4 + [vec_spec] * 2
                  + [dil_spec(d) for d in B_DILS for _ in range(3)]),
        out_specs=[out_spec, out_spec],
        out_shape=[jax.ShapeDtypeStruct((bsz, seq, A_WIDTH), BF16),
                   jax.ShapeDtypeStruct((bsz, seq, B_WIDTH), BF16)],
        scratch_shapes=([pltpu.VMEM((seq, 128), BF16),
                         pltpu.VMEM((seq, 128), F32), pltpu.VMEM((seq, 128), F32),
                         pltpu.VMEM((seq // A_CHUNK, A_DK, A_DK), F32),
                         pltpu.VMEM((seq // A_CHUNK, A_DK, A_DK), BF16)]
                        + [pltpu.VMEM((seq, 128), F32)] * 6),
        compiler_params=_cparams(2),
        name="mixers_ab",
    )(view(qa), view(fa), view(ia), view(ga), lb.reshape(4, 1, 128), onw.reshape(4, 1, 128),
      *qkv_views)


KV_BATCH = 4


def _kv_kernel(mem_ref, nw_ref, w_ref, k_ref, v_ref):
    rows = mem_ref.shape[0] * mem_ref.shape[1]
    h = _rms(mem_ref[...].reshape(rows, D_MODEL), nw_ref[...]).astype(BF16)
    k_ref[...] = _dot(h, w_ref[:, :D_MODEL]).astype(BF16).reshape(k_ref.shape)
    v_ref[...] = _dot(h, w_ref[:, D_MODEL:]).astype(BF16).reshape(v_ref.shape)


def _kv_proj(mem, nw, wkv):
    bsz = mem.shape[0]
    blk = pl.BlockSpec((KV_BATCH, MEM_LEN, D_MODEL), lambda b: (b, 0, 0))
    kv = jax.ShapeDtypeStruct((bsz, MEM_LEN, D_MODEL), BF16)
    return pl.pallas_call(
        _kv_kernel,
        grid=(bsz // KV_BATCH,),
        in_specs=[blk, _resident((1, D_MODEL)), _resident((D_MODEL, 2 * D_MODEL))],
        out_specs=[blk, blk],
        out_shape=[kv, kv],
        compiler_params=_cparams(1),
        name="xattn_kv",
    )(mem, nw, wkv)


def _tail_kernel(*refs, with_mixer_proj):
    if with_mixer_proj:
        oa_ref, ob_ref, w_mix_ref, nw_mix_ref = refs[:4]
        refs = refs[4:]
    (x_ref, k_ref, v_ref, wq_ref, wo_ref, nw2_ref, nw3_ref,
     nw4_ref, w_in_ref, w_out_ref, nw5_ref, o_ref) = refs
    sub = x_ref.shape[0] // TAIL_SUBTILES
    groups = [slice(g * sub, (g + 1) * sub) for g in range(TAIL_SUBTILES)]
    each = lambda fn, *lists: [fn(*args) for args in zip(*lists)]

    x = [x_ref[rows, :] for rows in groups]
    if with_mixer_proj:
        y = [_dot(oa_ref[rows, :], w_mix_ref[:A_WIDTH, :])
             + _dot(ob_ref[rows, :], w_mix_ref[A_WIDTH:, :]) for rows in groups]
        x = each(lambda xg, yg: xg + _rms(yg, nw_mix_ref[...]), x, y)

    h = each(lambda xg: _rms(xg, nw2_ref[...]).astype(BF16), x)
    q = each(lambda hg: _dot(hg, wq_ref[...]), h)
    heads = [[] for _ in groups]
    for hd in range(X_HEADS):
        cols = slice(hd * X_HD, (hd + 1) * X_HD)
        s = each(lambda qg: lax.dot_general((qg[:, cols] * (X_HD ** -0.5)).astype(BF16),
                                            k_ref[:, cols], NT_DIMS, preferred_element_type=F32), q)
        p = each(lambda sg: jnp.exp(sg - jnp.max(sg, axis=-1, keepdims=True)), s)
        pv = each(lambda pg: _dot(pg.astype(BF16), v_ref[:, cols]), p)
        for g, (pg, pvg) in enumerate(zip(p, pv)):
            heads[g].append((pvg / jnp.sum(pg, axis=-1, keepdims=True)).astype(BF16))
    y = each(lambda hg: _dot(jnp.concatenate(hg, axis=1), wo_ref[...]), heads)
    x = each(lambda xg, yg: xg + _rms(yg, nw3_ref[...]), x, y)

    h = each(lambda xg: _rms(xg, nw4_ref[...]).astype(BF16), x)
    y = [None] * len(groups)
    lo = 0
    for width in FF_CHUNKS:
        gate, up = slice(lo, lo + width), slice(D_FF + lo, D_FF + lo + width)
        z1 = each(lambda hg: _dot(hg, w_in_ref[:, gate]), h)
        z2 = each(lambda hg: _dot(hg, w_in_ref[:, up]), h)
        a = each(lambda z1g, z2g: (z1g * jax.nn.sigmoid(z1g) * z2g).astype(BF16), z1, z2)
        part = each(lambda ag: _dot(ag, w_out_ref[lo:lo + width, :]), a)
        y = [pg if yg is None else yg + pg for yg, pg in zip(y, part)]
        lo += width
    for rows, xg, yg in zip(groups, x, y):
        o_ref[rows, :] = xg + _rms(yg, nw5_ref[...])


def _row_spec(time_major, seq):
    if time_major:
        return pl.BlockSpec((TM, D_MODEL), lambda b, i: (i, b))
    tiles = seq // TM
    return pl.BlockSpec((TM, D_MODEL), lambda b, i: (b * tiles + i, 0))


def _stream_shape(time_major, bsz, seq):
    shape = (seq, bsz * D_MODEL) if time_major else (bsz * seq, D_MODEL)
    return jax.ShapeDtypeStruct(shape, F32)


def _layer_tail(xs, mixer, k, v, wq, wo, nw2, nw3, nw4, w_in, w_out, nw5, bsz, seq,
                in_time_major, out_time_major):
    tiles = seq // TM
    vec = _resident((1, D_MODEL))
    kvs = pl.BlockSpec((None, MEM_LEN, D_MODEL), lambda b, i: (b, 0, 0))
    in_specs = [_row_spec(in_time_major, seq), kvs, kvs,
                _resident((D_MODEL, D_MODEL)), _resident((D_MODEL, D_MODEL)), vec, vec,
                vec, _resident((D_MODEL, 2 * D_FF)), _resident((D_FF, D_MODEL)), vec]
    operands = [xs, k, v, wq, wo, nw2, nw3, nw4, w_in, w_out, nw5]
    if mixer is not None:
        half = pl.BlockSpec((TM, A_WIDTH), lambda b, i: (b * tiles + i, 0))
        in_specs = [half, half, _resident((D_MODEL, D_MODEL)), vec] + in_specs
        operands = list(mixer) + operands
    return pl.pallas_call(
        functools.partial(_tail_kernel, with_mixer_proj=mixer is not None),
        grid=(bsz, tiles),
        in_specs=in_specs,
        out_specs=_row_spec(out_time_major, seq),
        out_shape=_stream_shape(out_time_major, bsz, seq),
        compiler_params=_cparams(2),
        name="layer_tail",
    )(*operands)


S5_NB = 8
S5_TT = 64
S5_CB = 4
S5_BLK_STATE = 16 * 2 * C_STATE
S5_PAIRS = 8


def _s5_kernel(x_ref, nwa_ref, bt_ref, ct_ref, a_ref, d_ref, wg_ref, nwb_ref, o_ref,
               xs_s, v_s, h_s, y_s, st_s, os_s, ub_s):
    tt = x_ref.shape[0]
    rows = tt * S5_NB
    lane_slabs = D_MODEL // 128

    @pl.when(pl.program_id(1) == 0)
    def _():
        st_s[...] = jnp.zeros_like(st_s)

    for b in range(S5_NB):
        for k in range(lane_slabs):
            lanes = slice(b * D_MODEL + k * 128, b * D_MODEL + (k + 1) * 128)
            xs_s[pl.ds(k * rows + b, tt, stride=S5_NB), :] = x_ref[:, lanes]

    halves = [slice(0, rows // 2), slice(rows // 2, rows)]

    def load_x(hv):
        return jnp.concatenate([xs_s[k * rows + hv.start:k * rows + hv.stop, :]
                                for k in range(lane_slabs)], axis=1)

    ub_s[...] = _rms(load_x(slice(0, rows)), nwa_ref[...]).astype(BF16)

    def project(cb):
        v_s[cb % 2] = _dot(ub_s[:, cb * 256:(cb + 1) * 256], bt_ref[cb])

    project(0)
    for cb in range(S5_CB):
        vb, hb = v_s.at[cb % 2], h_s.at[cb % 2]
        if cb + 1 < S5_CB:
            project(cb + 1)
        for m in range(S5_PAIRS):
            ar = jnp.broadcast_to(a_ref[cb, m:m + 1, :], (S5_NB, 128))
            ai = jnp.broadcast_to(a_ref[cb, S5_PAIRS + m:S5_PAIRS + m + 1, :], (S5_NB, 128))
            c_re = slice(256 * m, 256 * m + 128)
            c_im = slice(256 * m + 128, 256 * m + 256)
            s_re = slice(cb * S5_BLK_STATE + 256 * m, cb * S5_BLK_STATE + 256 * m + 128)
            s_im = slice(cb * S5_BLK_STATE + 256 * m + 128, cb * S5_BLK_STATE + 256 * m + 256)
            hr, hi = st_s[:, s_re], st_s[:, s_im]
            for i in range(tt // 2):
                hist_r, hist_i = [], []
                for step in range(2):
                    rows_t = slice((2 * i + step) * S5_NB, (2 * i + step + 1) * S5_NB)
                    hr, hi = (ar * hr - ai * hi + vb[rows_t, c_re],
                              ar * hi + ai * hr + vb[rows_t, c_im])
                    hist_r.append(hr)
                    hist_i.append(hi)
                both = slice(2 * i * S5_NB, (2 * i + 2) * S5_NB)
                hb[both, c_re] = jnp.concatenate(hist_r, axis=0).astype(BF16)
                hb[both, c_im] = jnp.concatenate(hist_i, axis=0).astype(BF16)
            st_s[:, s_re] = hr
            st_s[:, s_im] = hi
        y_s[:, cb * 256:(cb + 1) * 256] = _dot(hb[...], ct_ref[cb])

    x = [load_x(hv) for hv in halves]
    gl = []
    for hv, xh in zip(halves, x):
        t = y_s[hv, :] + d_ref[...] * _rms(xh, nwa_ref[...])
        gl.append((0.5 * t * (1.0 + lax.erf(t * math.sqrt(0.5)))).astype(BF16))
    z = [_dot(g, wg_ref[...]) for g in gl]
    for hv, xh, zh in zip(halves, x, z):
        y = zh[:, :D_MODEL] * jax.nn.sigmoid(zh[:, D_MODEL:])
        out = xh + _rms(y, nwb_ref[...])
        for k in range(lane_slabs):
            os_s[k * rows + hv.start:k * rows + hv.stop, :] = out[:, k * 128:(k + 1) * 128]
    for b in range(S5_NB):
        for k in range(lane_slabs):
            lanes = slice(b * D_MODEL + k * 128, b * D_MODEL + (k + 1) * 128)
            o_ref[:, lanes] = os_s[pl.ds(k * rows + b, tt, stride=S5_NB), :]


def _s5_block(xs, nwa, bt, ct, avec, d, w_glu, nwb, bsz, seq, tt=S5_TT):
    rows = tt * S5_NB
    tile = pl.BlockSpec((tt, S5_NB * D_MODEL), lambda g, i: (i, g))
    return pl.pallas_call(
        _s5_kernel,
        grid=(bsz // S5_NB, seq // tt),
        in_specs=[tile, _resident((1, D_MODEL)), _resident(bt.shape), _resident(ct.shape),
                  _resident(avec.shape), _resident((1, D_MODEL)),
                  _resident((D_MODEL, 2 * D_MODEL)), _resident((1, D_MODEL))],
        out_specs=tile,
        out_shape=_stream_shape(True, bsz, seq),
        scratch_shapes=[pltpu.VMEM((D_MODEL // 128 * rows, 128), F32),
                        pltpu.VMEM((2, rows, S5_BLK_STATE), F32),
                        pltpu.VMEM((2, rows, S5_BLK_STATE), BF16),
                        pltpu.VMEM((rows, D_MODEL), F32),
                        pltpu.VMEM((S5_NB, S5_CB * S5_BLK_STATE), F32),
                        pltpu.VMEM((D_MODEL // 128 * rows, 128), F32),
                        pltpu.VMEM((rows, D_MODEL), BF16)],
        compiler_params=_cparams(2),
        name="s5_block",
    )(xs, nwa, bt, ct, avec, d, w_glu, nwb)


def _s5_weights(lam_re, lam_im, log_dt, b_re, b_im, c_re, c_im):
    lr = jnp.minimum(lam_re.astype(F32), C_MIN_NEG_RE)
    li = lam_im.astype(F32)
    dt = jnp.exp(log_dt.astype(F32))[:, None]
    mag = jnp.exp(dt * lr)
    ar, ai = mag * jnp.cos(dt * li), mag * jnp.sin(dt * li)
    den = lr * lr + li * li
    zr = ((ar - 1.0) * lr + ai * li) / den
    zi = (ai * lr - (ar - 1.0) * li) / den
    br, bi = b_re.astype(F32), b_im.astype(F32)
    bbr = zr[..., None] * br - zi[..., None] * bi
    bbi = zr[..., None] * bi + zi[..., None] * br
    cr, ci = c_re.astype(F32), c_im.astype(F32)
    split = lambda t: t.reshape((S5_CB, S5_PAIRS, 2) + t.shape[1:])
    chan_group = jnp.arange(256) // C_GROUP
    col = jnp.arange(S5_BLK_STATE)
    state_group = (col // 256) * 2 + (col // C_STATE) % 2
    same_group = chan_group[:, None] == state_group[None, :]
    bb = jnp.stack([split(bbr), split(bbi)], axis=0)
    strip = bb.transpose(1, 5, 2, 0, 3, 4).reshape(S5_CB, C_GROUP, S5_BLK_STATE)
    bt = jnp.where(same_group[None], jnp.tile(strip, (1, 256 // C_GROUP, 1)), 0.0)
    cc = jnp.stack([split(cr), -split(ci)], axis=0)
    strip = cc.transpose(1, 2, 0, 3, 5, 4).reshape(S5_CB, S5_BLK_STATE, C_GROUP)
    ct = jnp.where(same_group.T[None], jnp.tile(strip, (1, 1, 256 // C_GROUP)), 0.0)
    lanes = lambda t: t.reshape(S5_CB, S5_PAIRS, 2 * C_STATE)
    avec = jnp.concatenate([lanes(ar), lanes(ai)], axis=1)
    return bt.astype(BF16), ct.astype(BF16), avec


def _rotary_tables(seq):
    half = B_HD // 2
    inv_freq = ROPE_THETA ** (-jnp.arange(half, dtype=F32) / half)
    ang = jnp.arange(seq, dtype=jnp.int32).astype(F32)[:, None] * inv_freq[None, :]
    cos, sin = jnp.cos(ang), jnp.sin(ang)
    return jnp.concatenate([cos, cos], axis=-1), jnp.concatenate([-sin, sin], axis=-1)


def kernel(x, mem, norm_w, mem_norm_w, ab_w_in, ab_w_out, hgrn_lb_logits, hgrn_out_norm_w, s5_lambda_re, s5_lambda_im, s5_log_dt, s5_b_re, s5_b_im, s5_c_re, s5_c_im, s5_d, s5_w_glu, xattn_wq, xattn_wkv, xattn_wo, ffn_w_in, ffn_w_out):
    bsz, seq, _ = x.shape
    n = bsz * seq
    depth = norm_w.shape[0]
    nw = lambda layer, i: norm_w[layer, i].astype(F32).reshape(1, D_MODEL)
    lb_table = jnp.cumsum(jax.nn.softmax(hgrn_lb_logits.astype(F32), axis=0), axis=0)
    cos_t, sin_t = _rotary_tables(seq)
    xs = x.reshape(n, D_MODEL)
    for layer in range(depth):
        j = layer // 2
        time_major = layer % 2 == 1
        mixer = None
        if not time_major:
            outs = _proj_ab(xs, nw(layer, 0), ab_w_in[j].astype(BF16), cos_t, sin_t, bsz, seq)
            qa, fa, ia, ga = outs[:4]
            oa, ob = _mixers(qa, fa, ia, ga, lb_table[layer], hgrn_out_norm_w[j].astype(F32),
                             outs[4:], bsz, seq)
            mixer = (oa.reshape(n, A_WIDTH), ob.reshape(n, B_WIDTH),
                     ab_w_out[j].astype(BF16), nw(layer, 1))
        else:
            bt, ct, avec = _s5_weights(s5_lambda_re[j], s5_lambda_im[j], s5_log_dt[j],
                                       s5_b_re[j], s5_b_im[j], s5_c_re[j], s5_c_im[j])
            xs = _s5_block(xs, nw(layer, 0), bt, ct, avec,
                           s5_d[j].astype(F32).reshape(1, D_MODEL), s5_w_glu[j].astype(BF16),
                           nw(layer, 1), bsz, seq)
        k, v = _kv_proj(mem, mem_norm_w[layer].astype(F32).reshape(1, D_MODEL),
                        xattn_wkv[layer].astype(BF16))
        next_time_major = layer + 1 < depth and (layer + 1) % 2 == 1
        xs = _layer_tail(xs, mixer, k, v, xattn_wq[layer].astype(BF16),
                         xattn_wo[layer].astype(BF16), nw(layer, 2), nw(layer, 3), nw(layer, 4),
                         ffn_w_in[layer].astype(BF16), ffn_w_out[layer].astype(BF16),
                         nw(layer, 5), bsz, seq, time_major, next_time_major)
    return xs.reshape(bsz, seq, D_MODEL)
```
